```python
import jax, jax.numpy as jnp
from jax import lax
import numpy as np

D_MODEL = 1024
BATCH = 4
SEQ = 8192
DEPTH = 1
DEC_BATCH = 128
DEC_SEQ = 8
PAST_LEN = 16384
PAGE_SIZE = 128

HEAD_DIM = 64
D_ATTN = D_MODEL // 2
D_RWKV = D_MODEL - D_ATTN
N_HEADS = D_ATTN // HEAD_DIM
N_KV_HEADS = 2
GQA_GROUP = N_HEADS // N_KV_HEADS
D_KV = N_KV_HEADS * HEAD_DIM
WINDOW = 128
BLOCK = 128
ROPE_THETA = 10000.0
RWKV_HEADS = D_RWKV // HEAD_DIM
D_DECAY_LORA = 32
D_A_LORA = 32
D_GATE_LORA = 96
D_SHIFT = 3 * D_RWKV + D_DECAY_LORA + D_A_LORA + D_GATE_LORA
D_IN = D_ATTN + 2 * D_KV + D_SHIFT
D_FF = 2816
CONV_W = 3
RMS_EPS = 1e-6
GN_EPS = 64e-5
NEG_INF = -1e30

kernel_name = "hymba_swa_sink_rwkv7_convffn_step"


def _rms_norm(x, g):
    xf = x.astype(jnp.float32)
    y = xf * lax.rsqrt(jnp.mean(xf * xf, axis=-1, keepdims=True) + RMS_EPS)
    return (y * g.astype(jnp.float32)).astype(x.dtype)


def _rope(x, pos):
    half = HEAD_DIM // 2
    inv = ROPE_THETA ** (-jnp.arange(half, dtype=jnp.float32) / half)
    ang = pos.astype(jnp.float32)[:, None] * inv[None, :]
    cos = jnp.cos(ang)[None, :, None, :]
    sin = jnp.sin(ang)[None, :, None, :]
    xf = x.astype(jnp.float32)
    x1, x2 = xf[..., :half], xf[..., half:]
    return jnp.concatenate([x1 * cos - x2 * sin, x2 * cos + x1 * sin], axis=-1).astype(x.dtype)


def _attend_with_sinks(q, k, v, mask, sinks):
    s = jnp.einsum('...qkgd,...skd->...kgqs', q, k,
                   preferred_element_type=jnp.float32) * (HEAD_DIM ** -0.5)
    s = jnp.where(mask[..., None, None, :, :], s, NEG_INF)
    sink = sinks.astype(jnp.float32)[:, :, None, None]
    m = jnp.maximum(jnp.max(s, axis=-1, keepdims=True), sink)
    p = jnp.exp(s - m)
    p = (p / (jnp.sum(p, axis=-1, keepdims=True) + jnp.exp(sink - m))).astype(v.dtype)
    return jnp.einsum('...kgqs,...skd->...qkgd', p, v)


def _attn_prompt(q, k, v, sinks):
    B, T = q.shape[:2]
    nb = T // BLOCK
    qb = q.reshape(B, nb, BLOCK, N_KV_HEADS, GQA_GROUP, HEAD_DIM)
    kb = k.reshape(B, nb, BLOCK, N_KV_HEADS, HEAD_DIM)
    vb = v.reshape(B, nb, BLOCK, N_KV_HEADS, HEAD_DIM)
    pad = ((0, 0), (1, 0), (0, 0), (0, 0), (0, 0))
    kcat = jnp.concatenate([jnp.pad(kb, pad)[:, :-1], kb], axis=2)
    vcat = jnp.concatenate([jnp.pad(vb, pad)[:, :-1], vb], axis=2)
    qi = jnp.arange(BLOCK)[:, None]
    sj = jnp.arange(2 * BLOCK)[None, :]
    dist = qi + BLOCK - sj
    band = (dist >= 0) & (dist <= WINDOW)
    valid = (jnp.arange(nb)[:, None, None] > 0) | (sj >= BLOCK)[None]
    mask = band[None] & valid
    sk = sinks.reshape(N_KV_HEADS, GQA_GROUP)
    o = _attend_with_sinks(qb, kcat, vcat, mask, sk)
    return o.reshape(B, T, D_ATTN)


def _attn_sample(q, k, v, ck, cv, sinks):
    Bd, L = q.shape[:2]
    kcat = jnp.concatenate([ck, k], axis=1)
    vcat = jnp.concatenate([cv, v], axis=1)
    qi = jnp.arange(L)[:, None]
    sj = jnp.arange(WINDOW + L)[None, :]
    dist = qi + WINDOW - sj
    mask = (dist >= 0) & (dist <= WINDOW)
    sk = sinks.reshape(N_KV_HEADS, GQA_GROUP)
    o = _attend_with_sinks(q.reshape(Bd, L, N_KV_HEADS, GQA_GROUP, HEAD_DIM), kcat, vcat, mask, sk)
    return o.reshape(Bd, L, D_ATTN), kcat[:, -WINDOW:], vcat[:, -WINDOW:]


def _rwkv7(h, shift_prev, s0, mu, w0, w_decay_up, a0, w_a_up, w_g_up, k_k, k_a, r_k, gn_w, gn_b):
    B, T, _ = h.shape
    f32 = jnp.float32
    hprev = jnp.concatenate([shift_prev[:, None, :], h[:, :-1]], axis=1)
    hs = h + (hprev - h) * mu
    cuts = [D_RWKV, 2 * D_RWKV, 3 * D_RWKV, 3 * D_RWKV + D_DECAY_LORA,
            3 * D_RWKV + D_DECAY_LORA + D_A_LORA]
    r, k, v, wd, ad, gd = jnp.split(hs, cuts, axis=-1)
    w = -jax.nn.softplus(-(w0 + jnp.tanh(wd) @ w_decay_up)) - 0.5
    decay = jnp.exp(-jnp.exp(w.astype(f32)))
    a = jax.nn.sigmoid(a0 + ad @ w_a_up)
    g = jax.nn.sigmoid(gd) @ w_g_up
    heads = lambda t: t.astype(f32).reshape(B, T, RWKV_HEADS, HEAD_DIM)
    kk = heads(k * k_k)
    kk = kk * lax.rsqrt(jnp.maximum(jnp.sum(kk * kk, axis=-1, keepdims=True), 1e-24))
    k = k * (1.0 + (a - 1.0) * k_a)
    r_h, k_h, v_h, a_h, w_h = heads(r), heads(k), heads(v), heads(a), heads(decay)
    xs = tuple(jnp.moveaxis(t, 1, 0) for t in (r_h, w_h, k_h, v_h, kk, a_h))

    def step(S, inp):
        r_t, w_t, k_t, v_t, kk_t, a_t = inp
        sa = jnp.einsum('bhij,bhj->bhi', S, kk_t)
        S = (S * w_t[:, :, None, :] - sa[..., None] * (kk_t * a_t)[:, :, None, :]
             + v_t[..., None] * k_t[:, :, None, :])
        return S, jnp.einsum('bhij,bhj->bhi', S, r_t)

    S_T, ys = lax.scan(step, s0.astype(f32), xs)
    y = jnp.moveaxis(ys, 0, 1)
    mean = jnp.mean(y, axis=-1, keepdims=True)
    var = jnp.mean(jnp.square(y - mean), axis=-1, keepdims=True)
    y = ((y - mean) * lax.rsqrt(var + GN_EPS)).reshape(B, T, D_RWKV) * gn_w + gn_b
    bonus = jnp.sum(r_h * k_h * r_k.astype(f32), axis=-1, keepdims=True) * v_h
    y = (y + bonus.reshape(B, T, D_RWKV)) * g
    return y.astype(h.dtype), h[:, -1], S_T.astype(s0.dtype)


def _conv_ffn(x, conv_prev, w_ffn_in, conv_w, conv_b, w_ffn_out):
    T = x.shape[1]
    gu = x @ w_ffn_in
    z, u = gu[..., :D_FF], gu[..., D_FF:]
    zp = jnp.concatenate([conv_prev, z], axis=1)
    zc = conv_b + conv_w[0] * zp[:, 0:T]
    for j in range(1, CONV_W):
        zc = zc + conv_w[j] * zp[:, j:j + T]
    hid = jax.nn.silu(zc) * u
    return hid @ w_ffn_out, zp[:, -(CONV_W - 1):]


def _layer(x, pos, ck, cv, shift_prev, s0, conv_prev,
           g_pre_mix, w_in, attn_sinks, mu_shift, w0, w_decay_up, a0, w_a_up, w_g_up,
           k_k, k_a, r_k, gn_w, gn_b, w_out, g_post_mix, g_pre_ffn, w_ffn_in, conv_w,
           conv_b, w_ffn_out, g_post_ffn):
    B, T, _ = x.shape
    hn = _rms_norm(x, g_pre_mix)
    proj = hn @ w_in
    q = proj[..., :D_ATTN].reshape(B, T, N_HEADS, HEAD_DIM)
    k = proj[..., D_ATTN:D_ATTN + D_KV].reshape(B, T, N_KV_HEADS, HEAD_DIM)
    v = proj[..., D_ATTN + D_KV:D_ATTN + 2 * D_KV].reshape(B, T, N_KV_HEADS, HEAD_DIM)
    h_rw = proj[..., D_ATTN + 2 * D_KV:]
    q = _rope(q, pos)
    k = _rope(k, pos)
    if ck is None:
        o_attn = _attn_prompt(q, k, v, attn_sinks)
        new_k, new_v = k[:, -WINDOW:], v[:, -WINDOW:]
    else:
        o_attn, new_k, new_v = _attn_sample(q, k, v, ck, cv, attn_sinks)
    o_rw, new_shift, new_S = _rwkv7(h_rw, shift_prev, s0, mu_shift, w0, w_decay_up, a0,
                                    w_a_up, w_g_up, k_k, k_a, r_k, gn_w, gn_b)
    mix = jnp.concatenate([o_attn, o_rw], axis=-1) @ w_out
    x = x + _rms_norm(mix, g_post_mix)
    f, new_conv = _conv_ffn(_rms_norm(x, g_pre_ffn), conv_prev, w_ffn_in, conv_w, conv_b, w_ffn_out)
    x = x + _rms_norm(f, g_post_ffn)
    return x, (new_k, new_v, new_shift, new_S, new_conv)


def setup_inputs(seed: int = 0) -> dict:
    key = jax.random.key(seed)
    ks = jax.random.split(key, 32)
    nrm = lambda i, shape, s: jax.random.normal(ks[i], shape, jnp.float32) * s
    unif = lambda i, shape, lo, hi: jax.random.uniform(ks[i], shape, jnp.float32, lo, hi)
    L = DEPTH
    return {
        "x_prompt": nrm(0, (BATCH, SEQ, D_MODEL), 1.0),
        "x_sample": nrm(1, (DEC_BATCH, DEC_SEQ, D_MODEL), 1.0),
        "cache_k_win": nrm(2, (L, DEC_BATCH, WINDOW, N_KV_HEADS, HEAD_DIM), 1.0),
        "cache_v_win": nrm(3, (L, DEC_BATCH, WINDOW, N_KV_HEADS, HEAD_DIM), 1.0),
        "state_shift": nrm(4, (L, DEC_BATCH, D_SHIFT), 1.0),
        "state_wkv": nrm(5, (L, DEC_BATCH, RWKV_HEADS, HEAD_DIM, HEAD_DIM), 0.3),
        "state_conv": nrm(6, (L, DEC_BATCH, CONV_W - 1, D_FF), 1.0),
        "g_pre_mix": 1.0 + nrm(7, (L, D_MODEL), 0.05),
        "w_in": nrm(8, (L, D_MODEL, D_IN), D_MODEL ** -0.5),
        "attn_sinks": nrm(9, (L, N_HEADS), 0.5),
        "mu_shift": unif(10, (L, D_SHIFT), 0.0, 1.0),
        "w0": unif(11, (L, D_RWKV), -1.5, 1.5),
        "w_decay_up": nrm(12, (L, D_DECAY_LORA, D_RWKV), 0.1),
        "a0": nrm(13, (L, D_RWKV), 0.1),
        "w_a_up": nrm(14, (L, D_A_LORA, D_RWKV), 0.1),
        "w_g_up": nrm(15, (L, D_GATE_LORA, D_RWKV), D_GATE_LORA ** -0.5),
        "k_k": 0.85 + nrm(16, (L, D_RWKV), 0.05),
        "k_a": 1.0 + nrm(17, (L, D_RWKV), 0.05),
        "r_k": nrm(18, (L, RWKV_HEADS, HEAD_DIM), 0.1),
        "gn_w": 1.0 + nrm(19, (L, D_RWKV), 0.05),
        "gn_b": nrm(20, (L, D_RWKV), 0.01),
        "w_out": nrm(21, (L, D_MODEL, D_MODEL), D_MODEL ** -0.5),
        "g_post_mix": 1.0 + nrm(22, (L, D_MODEL), 0.05),
        "g_pre_ffn": 1.0 + nrm(23, (L, D_MODEL), 0.05),
        "w_ffn_in": nrm(24, (L, D_MODEL, 2 * D_FF), D_MODEL ** -0.5),
        "conv_w": nrm(25, (L, CONV_W, D_FF), CONV_W ** -0.5),
        "conv_b": nrm(26, (L, D_FF), 0.01),
        "w_ffn_out": nrm(27, (L, D_FF, D_MODEL), D_FF ** -0.5),
        "g_post_ffn": 1.0 + nrm(28, (L, D_MODEL), 0.05),
    }


def reference(x_prompt, x_sample, cache_k_win, cache_v_win, state_shift, state_wkv, state_conv,
              g_pre_mix, w_in, attn_sinks, mu_shift, w0, w_decay_up, a0, w_a_up, w_g_up,
              k_k, k_a, r_k, gn_w, gn_b, w_out, g_post_mix, g_pre_ffn, w_ffn_in, conv_w,
              conv_b, w_ffn_out, g_post_ffn):
    B, T = x_prompt.shape[:2]
    pos_p = jnp.arange(T, dtype=jnp.int32)
    pos_s = PAST_LEN + jnp.arange(x_sample.shape[1], dtype=jnp.int32)
    xp, xs = x_prompt, x_sample
    st_p, st_s = [], []
    for l in range(DEPTH):
        lw = (g_pre_mix[l], w_in[l], attn_sinks[l], mu_shift[l], w0[l], w_decay_up[l], a0[l],
              w_a_up[l], w_g_up[l], k_k[l], k_a[l], r_k[l], gn_w[l], gn_b[l], w_out[l],
              g_post_mix[l], g_pre_ffn[l], w_ffn_in[l], conv_w[l], conv_b[l], w_ffn_out[l],
              g_post_ffn[l])
        zero_shift = jnp.zeros((B, D_SHIFT), xp.dtype)
        zero_wkv = jnp.zeros((B, RWKV_HEADS, HEAD_DIM, HEAD_DIM), xp.dtype)
        zero_conv = jnp.zeros((B, CONV_W - 1, D_FF), xp.dtype)
        xp, sp = _layer(xp, pos_p, None, None, zero_shift, zero_wkv, zero_conv, *lw)
        xs, ss = _layer(xs, pos_s, cache_k_win[l], cache_v_win[l], state_shift[l],
                        state_wkv[l], state_conv[l], *lw)
        st_p.append(sp)
        st_s.append(ss)
    stk = lambda outs, i: jnp.stack([o[i] for o in outs], axis=0)
    return (xp, xs,
            stk(st_p, 0), stk(st_p, 1), stk(st_p, 2), stk(st_p, 3), stk(st_p, 4),
            stk(st_s, 0), stk(st_s, 1), stk(st_s, 2), stk(st_s, 3), stk(st_s, 4))
```

```python
import functools
import math

import jax
import jax.numpy as jnp
from jax import lax
from jax.experimental import pallas as pl
from jax.experimental.pallas import tpu as pltpu

f32 = jnp.float32
bf16 = jnp.bfloat16

D_MODEL = 1024
HEAD_DIM = 64
D_ATTN = 512
D_RWKV = 512
N_HEADS = 8
N_KV_HEADS = 2
GQA_GROUP = 4
D_KV = 128
WINDOW = 128
ROPE_THETA = 10000.0
RWKV_HEADS = 8
D_DECAY_LORA = 32
D_A_LORA = 32
D_GATE_LORA = 96
D_LORA = D_DECAY_LORA + D_A_LORA + D_GATE_LORA
D_LORA_PAD = 256
D_SHIFT = 3 * D_RWKV + D_LORA
D_QKV = D_ATTN + 2 * D_KV
D_FF = 2816
FF_CHUNK = 256
N_FF_CHUNKS = D_FF // FF_CHUNK
CONV_W = 3
PAST_LEN = 16384
RMS_EPS = 1e-6
GN_EPS = 64e-5
NEG_INF = -1e30
LANES = 128
SUBLANES = 8
RWKV_CHUNK = 64
VMEM_LIMIT = 56 * 1024 * 1024


def _const_spec(shape):
    nd = len(shape)
    return pl.BlockSpec(shape, lambda *_: (0,) * nd, pipeline_mode=pl.Buffered(1))


def _rms(x, g):
    ms = jnp.mean(x * x, axis=-1, keepdims=True)
    return x * lax.rsqrt(ms + RMS_EPS) * g


def _dot(a, b):
    return jnp.dot(a, b, preferred_element_type=f32)


def _dot_nt(a, b):
    return lax.dot_general(a, b, (((1,), (1,)), ((), ())), preferred_element_type=f32)


def _dot_tn(a, b):
    return lax.dot_general(a, b, (((0,), (0,)), ((), ())), preferred_element_type=f32)


def _group_sum(x, bd):
    hi = x.astype(bf16)
    r1 = x - hi.astype(f32)
    mid = r1.astype(bf16)
    lo = (r1 - mid.astype(f32)).astype(bf16)
    return _dot(hi, bd) + _dot(mid, bd) + _dot(lo, bd)


def _in_proj_kernel(x_ref, g_ref, wqkv_ref, wrkv_ref, wlora_ref, cos_ref, sin_ref,
                    qkv_ref, rkv_ref, lora_ref):
    hn = _rms(x_ref[...], g_ref[...]).astype(bf16)
    qkv = _dot(hn, wqkv_ref[...])
    cos = cos_ref[...]
    sin = sin_ref[...]
    lane = lax.broadcasted_iota(jnp.int32, cos.shape, 1)
    first_half = (lane % HEAD_DIM) < (HEAD_DIM // 2)
    for j in range((D_ATTN + D_KV) // LANES):
        xs = qkv[:, j * LANES:(j + 1) * LANES]
        partner = jnp.where(first_half,
                            pltpu.roll(xs, LANES - HEAD_DIM // 2, 1),
                            pltpu.roll(xs, HEAD_DIM // 2, 1))
        qkv_ref[:, j * LANES:(j + 1) * LANES] = xs * cos + partner * sin
    qkv_ref[:, D_ATTN + D_KV:] = qkv[:, D_ATTN + D_KV:]
    rkv_ref[...] = _dot(hn, wrkv_ref[...])
    lora_ref[...] = _dot(hn, wlora_ref[...])


def _in_proj(x2d, g, wqkv, wrkv, wlora, cos, sin, tm):
    n = x2d.shape[0]
    n_pos = cos.shape[0] // tm
    row = lambda w: pl.BlockSpec((tm, w), lambda i: (i, 0))
    pos = pl.BlockSpec((tm, LANES), lambda i: (i % n_pos, 0))
    return pl.pallas_call(
        _in_proj_kernel,
        grid=(n // tm,),
        in_specs=[row(D_MODEL), _const_spec((1, D_MODEL)), _const_spec(wqkv.shape),
                  _const_spec(wrkv.shape), _const_spec(wlora.shape), pos, pos],
        out_specs=[row(D_QKV), row(3 * D_RWKV), row(D_LORA_PAD)],
        out_shape=[jax.ShapeDtypeStruct((n, D_QKV), f32),
                   jax.ShapeDtypeStruct((n, 3 * D_RWKV), f32),
                   jax.ShapeDtypeStruct((n, D_LORA_PAD), f32)],
        compiler_params=pltpu.CompilerParams(
            dimension_semantics=("arbitrary",), vmem_limit_bytes=VMEM_LIMIT),
        name="in_proj",
    )(x2d, g, wqkv, wrkv, wlora, cos, sin)


def _softmax_sink(s, sink_col):
    m = jnp.maximum(jnp.max(s, axis=-1, keepdims=True), sink_col)
    p = jnp.exp(s - m)
    den = jnp.sum(p, axis=-1, keepdims=True) + jnp.exp(sink_col - m)
    return p / den


def _sink_column(sink_ref, kh, rows_per_head, shape):
    row = lax.broadcasted_iota(jnp.int32, shape, 0)
    col = jnp.full(shape, sink_ref[kh * GQA_GROUP + GQA_GROUP - 1], f32)
    for g in range(GQA_GROUP - 2, -1, -1):
        col = jnp.where(row < (g + 1) * rows_per_head, sink_ref[kh * GQA_GROUP + g], col)
    return col


def _attn_prompt_kernel(sink_ref, q_ref, kvc_ref, kvp_ref, o_ref):
    blk = q_ref.shape[1]
    i = pl.program_id(1)
    q = q_ref[0] * (HEAD_DIM ** -0.5)
    kvc = kvc_ref[0]
    kvp = kvp_ref[0]
    shape = (GQA_GROUP * blk, 2 * blk)
    qi = lax.broadcasted_iota(jnp.int32, shape, 0) % blk
    sj = lax.broadcasted_iota(jnp.int32, shape, 1)
    dist = qi + blk - sj
    mask = (dist >= 0) & (dist <= WINDOW) & ((sj >= blk) | (i > 0))
    for kh in range(N_KV_HEADS):
        ks = slice(kh * HEAD_DIM, (kh + 1) * HEAD_DIM)
        vs = slice(D_KV + kh * HEAD_DIM, D_KV + (kh + 1) * HEAD_DIM)
        kcat = jnp.concatenate([kvp[:, ks], kvc[:, ks]], axis=0).astype(bf16)
        vcat = jnp.concatenate([kvp[:, vs], kvc[:, vs]], axis=0).astype(bf16)
        qs = jnp.concatenate(
            [q[:, (kh * GQA_GROUP + g) * HEAD_DIM:(kh * GQA_GROUP + g + 1) * HEAD_DIM]
             for g in range(GQA_GROUP)], axis=0).astype(bf16)
        s = jnp.where(mask, _dot_nt(qs, kcat), NEG_INF)
        p = _softmax_sink(s, _sink_column(sink_ref, kh, blk, (GQA_GROUP * blk, 1)))
        o = _dot(p.astype(bf16), vcat)
        for g in range(GQA_GROUP):
            h = kh * GQA_GROUP + g
            o_ref[0, :, h * HEAD_DIM:(h + 1) * HEAD_DIM] = o[g * blk:(g + 1) * blk]


def _attn_prompt(sinks, qkv3, blk):
    b, t, _ = qkv3.shape
    return pl.pallas_call(
        _attn_prompt_kernel,
        grid=(b, t // blk),
        in_specs=[pl.BlockSpec(memory_space=pltpu.SMEM),
                  pl.BlockSpec((1, blk, D_ATTN), lambda bi, i: (bi, i, 0)),
                  pl.BlockSpec((1, blk, 2 * D_KV), lambda bi, i: (bi, i, D_ATTN // (2 * D_KV))),
                  pl.BlockSpec((1, blk, 2 * D_KV),
                               lambda bi, i: (bi, jnp.maximum(i - 1, 0), D_ATTN // (2 * D_KV)))],
        out_specs=pl.BlockSpec((1, blk, D_ATTN), lambda bi, i: (bi, i, 0)),
        out_shape=jax.ShapeDtypeStruct((b, t, D_ATTN), f32),
        compiler_params=pltpu.CompilerParams(
            dimension_semantics=("arbitrary", "arbitrary"), vmem_limit_bytes=VMEM_LIMIT),
        name="attn_prompt",
    )(sinks, qkv3, qkv3, qkv3)


def _attn_sample_kernel(sink_ref, qkv_ref, ck_ref, cv_ref, o_ref, nk_ref, nv_ref):
    bb, L, _ = qkv_ref.shape
    qkv = qkv_ref[...]
    q = qkv[:, :, :D_ATTN] * (HEAD_DIM ** -0.5)
    kn = qkv[:, :, D_ATTN:D_ATTN + D_KV]
    vn = qkv[:, :, D_ATTN + D_KV:]
    ck = ck_ref[...]
    cv = cv_ref[...]
    nk_ref[:, :WINDOW - L, :] = ck[:, L:, :]
    nk_ref[:, WINDOW - L:, :] = kn
    nv_ref[:, :WINDOW - L, :] = cv[:, L:, :]
    nv_ref[:, WINDOW - L:, :] = vn
    rows = GQA_GROUP * L
    qi = lax.broadcasted_iota(jnp.int32, (rows, WINDOW), 0) % L
    sj = lax.broadcasted_iota(jnp.int32, (rows, WINDOW), 1)
    mask_c = (sj >= qi)[None]
    qn = lax.broadcasted_iota(jnp.int32, (rows, L), 0) % L
    jn = lax.broadcasted_iota(jnp.int32, (rows, L), 1)
    mask_n = (jn <= qn)[None]
    for kh in range(N_KV_HEADS):
        ks = slice(kh * HEAD_DIM, (kh + 1) * HEAD_DIM)
        qs = jnp.concatenate(
            [q[:, :, (kh * GQA_GROUP + g) * HEAD_DIM:(kh * GQA_GROUP + g + 1) * HEAD_DIM]
             for g in range(GQA_GROUP)], axis=1)
        qsb = qs.astype(bf16)
        s_c = jnp.einsum('bqd,bkd->bqk', qsb, ck[:, :, ks].astype(bf16),
                         preferred_element_type=f32)
        s_n = jnp.einsum('bqd,bkd->bqk', qsb.astype(f32), kn[:, :, ks].astype(bf16).astype(f32),
                         preferred_element_type=f32, precision=lax.Precision.HIGHEST)
        s_c = jnp.where(mask_c, s_c, NEG_INF)
        s_n = jnp.where(mask_n, s_n, NEG_INF)
        sink_col = _sink_column(sink_ref, kh, L, (rows, 1))[None]
        m = jnp.maximum(jnp.maximum(jnp.max(s_c, axis=-1, keepdims=True),
                                    jnp.max(s_n, axis=-1, keepdims=True)), sink_col)
        p_c = jnp.exp(s_c - m)
        p_n = jnp.exp(s_n - m)
        den = (jnp.sum(p_c, axis=-1, keepdims=True) + jnp.sum(p_n, axis=-1, keepdims=True)
               + jnp.exp(sink_col - m))
        p_c = (p_c / den).astype(bf16)
        p_n = (p_n / den).astype(bf16).astype(f32)
        o = jnp.einsum('bqk,bkd->bqd', p_c, cv[:, :, ks].astype(bf16), preferred_element_type=f32)
        o = o + jnp.einsum('bqk,bkd->bqd', p_n, vn[:, :, ks].astype(bf16).astype(f32),
                           preferred_element_type=f32, precision=lax.Precision.HIGHEST)
        for g in range(GQA_GROUP):
            h = kh * GQA_GROUP + g
            o_ref[:, :, h * HEAD_DIM:(h + 1) * HEAD_DIM] = o[:, g * L:(g + 1) * L, :]


def _attn_sample(sinks, qkv3, ck, cv, bb):
    b, L, _ = qkv3.shape
    blk3 = lambda r, w: pl.BlockSpec((bb, r, w), lambda i: (i, 0, 0))
    return pl.pallas_call(
        _attn_sample_kernel,
        grid=(b // bb,),
        in_specs=[pl.BlockSpec(memory_space=pltpu.SMEM), blk3(L, D_QKV),
                  blk3(WINDOW, D_KV), blk3(WINDOW, D_KV)],
        out_specs=[blk3(L, D_ATTN), blk3(WINDOW, D_KV), blk3(WINDOW, D_KV)],
        out_shape=[jax.ShapeDtypeStruct((b, L, D_ATTN), f32),
                   jax.ShapeDtypeStruct((b, WINDOW, D_KV), f32),
                   jax.ShapeDtypeStruct((b, WINDOW, D_KV), f32)],
        compiler_params=pltpu.CompilerParams(
            dimension_semantics=("arbitrary",), vmem_limit_bytes=VMEM_LIMIT),
        name="attn_sample",
    )(sinks, qkv3, ck, cv)


def _rwkv_prep_body(h_rkv, h_lora, p_rkv, p_lora, mu_rkv, mu_lora, wl, w0, a0, kk_w, ka_w, bd,
                    r_ref, lw_ref, k_ref, v_ref, kk_ref, ba_ref, g_ref):
    hs = h_rkv + (p_rkv - h_rkv) * mu_rkv
    ls = h_lora + (p_lora - h_lora) * mu_lora
    r = hs[:, :D_RWKV]
    k = hs[:, D_RWKV:2 * D_RWKV]
    v = hs[:, 2 * D_RWKV:]
    lane = lax.broadcasted_iota(jnp.int32, ls.shape, 1)
    act = jnp.where(lane < D_DECAY_LORA, jnp.tanh(ls),
                    jnp.where(lane < D_DECAY_LORA + D_A_LORA, ls, jax.nn.sigmoid(ls)))
    pre = _dot(act.astype(bf16), wl)
    xw = -(w0 + pre[:, :D_RWKV])
    softplus = jnp.maximum(xw, 0.0) + jnp.log1p(jnp.exp(-jnp.abs(xw)))
    w = -softplus - 0.5
    lw_ref[...] = -jnp.exp(w)
    a = jax.nn.sigmoid(a0 + pre[:, D_RWKV:2 * D_RWKV])
    g_ref[...] = pre[:, 2 * D_RWKV:]
    kkp = k * kk_w
    kk = kkp * lax.rsqrt(jnp.maximum(_group_sum(kkp * kkp, bd), 1e-24))
    r_ref[...] = r
    k_ref[...] = k * (1.0 + (a - 1.0) * ka_w)
    v_ref[...] = v
    kk_ref[...] = kk
    ba_ref[...] = kk * a


def _shift_rows(x, first_row):
    row = lax.broadcasted_iota(jnp.int32, x.shape, 0)
    return jnp.where(row == 0, first_row, pltpu.roll(x, 1, 0))


def _rwkv_prep_prompt_kernel(rkv_ref, lora_ref, prkv_ref, plora_ref, s_rkv_ref, s_lora_ref,
                             mu_rkv_ref, mu_lora_ref, wl_ref, w0_ref, a0_ref, kkw_ref, kaw_ref,
                             bd_ref, *out_refs):
    i = pl.program_id(1)
    h_rkv = rkv_ref[...]
    h_lora = lora_ref[...]
    first = i == 0
    prev_rkv = jnp.where(first, s_rkv_ref[0], prkv_ref[SUBLANES - 1:SUBLANES, :])
    prev_lora = jnp.where(first, s_lora_ref[0], plora_ref[SUBLANES - 1:SUBLANES, :])
    _rwkv_prep_body(h_rkv, h_lora, _shift_rows(h_rkv, prev_rkv), _shift_rows(h_lora, prev_lora),
                    mu_rkv_ref[...], mu_lora_ref[...], wl_ref[...], w0_ref[...], a0_ref[...],
                    kkw_ref[...], kaw_ref[...], bd_ref[...], *out_refs)


def _rwkv_prep_sample_kernel(rkv_ref, lora_ref, c_rkv_ref, c_lora_ref, mu_rkv_ref, mu_lora_ref,
                             wl_ref, w0_ref, a0_ref, kkw_ref, kaw_ref, bd_ref, *out_refs, seq):
    h_rkv = rkv_ref[...]
    h_lora = lora_ref[...]

    def prev(h, cand):
        t = lax.broadcasted_iota(jnp.int32, h.shape, 0) % seq
        return jnp.where(t == 0, cand, pltpu.roll(h, 1, 0))

    _rwkv_prep_body(h_rkv, h_lora, prev(h_rkv, c_rkv_ref[...]), prev(h_lora, c_lora_ref[...]),
                    mu_rkv_ref[...], mu_lora_ref[...], wl_ref[...], w0_ref[...], a0_ref[...],
                    kkw_ref[...], kaw_ref[...], bd_ref[...], *out_refs)


def _rwkv_prep_prompt(rkv, lora, s_rkv, s_lora, consts, b, t, tm):
    nt = t // tm
    per8 = tm // SUBLANES
    row = lambda w: pl.BlockSpec((tm, w), lambda bi, i: (bi * nt + i, 0))
    prow = lambda w: pl.BlockSpec(
        (SUBLANES, w), lambda bi, i: (jnp.maximum((bi * nt + i) * per8 - 1, 0), 0))
    st = lambda w: pl.BlockSpec((1, 1, w), lambda bi, i: (bi, 0, 0))
    n = b * t
    return pl.pallas_call(
        _rwkv_prep_prompt_kernel,
        grid=(b, nt),
        in_specs=[row(3 * D_RWKV), row(D_LORA_PAD), prow(3 * D_RWKV), prow(D_LORA_PAD),
                  st(3 * D_RWKV), st(D_LORA_PAD)] + [_const_spec(c.shape) for c in consts],
        out_specs=[row(D_RWKV)] * 7,
        out_shape=[jax.ShapeDtypeStruct((n, D_RWKV), f32)] * 7,
        compiler_params=pltpu.CompilerParams(
            dimension_semantics=("arbitrary", "arbitrary"), vmem_limit_bytes=VMEM_LIMIT),
        name="rwkv_prep_prompt",
    )(rkv, lora, rkv, lora, s_rkv, s_lora, *consts)


def _rwkv_prep_sample(rkv, lora, c_rkv, c_lora, consts, seq, tm):
    n = rkv.shape[0]
    row = lambda w: pl.BlockSpec((tm, w), lambda i: (i, 0))
    return pl.pallas_call(
        functools.partial(_rwkv_prep_sample_kernel, seq=seq),
        grid=(n // tm,),
        in_specs=[row(3 * D_RWKV), row(D_LORA_PAD), row(3 * D_RWKV), row(D_LORA_PAD)]
        + [_const_spec(c.shape) for c in consts],
        out_specs=[row(D_RWKV)] * 7,
        out_shape=[jax.ShapeDtypeStruct((n, D_RWKV), f32)] * 7,
        compiler_params=pltpu.CompilerParams(
            dimension_semantics=("arbitrary",), vmem_limit_bytes=VMEM_LIMIT),
        name="rwkv_prep_sample",
    )(rkv, lora, c_rkv, c_lora, *consts)


def _rwkv_scan_kernel(r_ref, lw_ref, k_ref, v_ref, kk_ref, ba_ref, g_ref, s0_ref,
                      rk_ref, gnw_ref, gnb_ref, bd_ref, o_ref, st_ref, s_scr, y_scr, *, chunk):
    tt = r_ref.shape[0]
    nch = tt // chunk
    c2 = 2 * chunk
    n_pairs = D_RWKV // LANES
    i = pl.program_id(1)

    @pl.when(i == 0)
    def _():
        s_scr[...] = s0_ref[0]

    ti = lax.broadcasted_iota(jnp.int32, (chunk, chunk), 0)
    tj = lax.broadcasted_iota(jnp.int32, (chunk, chunk), 1)
    tri = (ti >= tj).astype(bf16)
    ri = lax.broadcasted_iota(jnp.int32, (c2, c2), 0)
    rj = lax.broadcasted_iota(jnp.int32, (c2, c2), 1)
    strict = (ri % chunk) > (rj % chunk)
    incl = (ri % chunk) >= (rj % chunk)
    eye2 = (ri == rj).astype(f32)
    li = lax.broadcasted_iota(jnp.int32, (LANES, LANES), 0)
    lj = lax.broadcasted_iota(jnp.int32, (LANES, LANES), 1)
    eye_l = li == lj
    low_half = lax.broadcasted_iota(jnp.int32, (chunk, LANES), 1) < HEAD_DIM

    def stack(x):
        return jnp.concatenate([jnp.where(low_half, x, 0.0), jnp.where(low_half, 0.0, x)], axis=0)

    def chunk_step(c, carry):
        rows = pl.ds(pl.multiple_of(c * chunk, chunk), chunk)
        lw = lw_ref[rows, :]
        hi = lw.astype(bf16)
        r1 = lw - hi.astype(f32)
        mid = r1.astype(bf16)
        lo = (r1 - mid.astype(f32)).astype(bf16)
        cl = _dot(tri, hi) + _dot(tri, mid) + _dot(tri, lo)
        e_fwd = jnp.exp(cl)
        e_inv = jnp.exp(-cl)
        pc = e_fwd[chunk - 1:chunk, :]
        rt_all = r_ref[rows, :] * e_fwd
        at_all = -kk_ref[rows, :] * jnp.exp(cl - lw)
        bt_all = ba_ref[rows, :] * e_inv
        kt_all = k_ref[rows, :] * e_inv
        v_all = v_ref[rows, :]
        for p in range(n_pairs):
            ls = slice(p * LANES, (p + 1) * LANES)
            rt_f = stack(rt_all[:, ls])
            rt = rt_f.astype(bf16)
            at = stack(at_all[:, ls]).astype(bf16)
            bt_f = stack(bt_all[:, ls])
            kt_f = stack(kt_all[:, ls])
            bt = bt_f.astype(bf16)
            kt = kt_f.astype(bf16)
            pcp = pc[:, ls]
            bh = (bt_f * pcp).astype(bf16)
            kh = (kt_f * pcp).astype(bf16)
            vs = stack(v_all[:, ls]).astype(bf16)
            aa = _dot_nt(jnp.concatenate([at, rt], axis=0), jnp.concatenate([bt, kt], axis=0))
            lab = jnp.where(strict, aa[:c2, :c2], 0.0)
            lak = jnp.where(strict, aa[:c2, c2:], 0.0)
            arb = jnp.where(incl, aa[c2:, :c2], 0.0).astype(bf16)
            ark = jnp.where(incl, aa[c2:, c2:], 0.0).astype(bf16)
            tm = eye2 + lab
            lp = lab
            for _ in range(int(math.log2(chunk)) - 1):
                lpb = lp.astype(bf16)
                lp = _dot(lpb, lpb)
                tm = tm + _dot(tm.astype(bf16), lp.astype(bf16))
            x = _dot(lak.astype(bf16), vs)
            wu = _dot(tm.astype(bf16), jnp.concatenate([at, x.astype(bf16)], axis=1)).astype(bf16)
            mg = _dot_tn(wu, bh)
            m = jnp.where(eye_l, pcp, 0.0) + mg[:LANES]
            gm = mg[LANES:] + _dot_tn(vs, kh)
            qy = _dot(arb, wu)
            q = rt_f + qy[:, :LANES]
            y0 = qy[:, LANES:] + _dot(ark, vs)
            s = s_scr[p]
            sb = s.astype(bf16)
            y = _dot_nt(q.astype(bf16), sb) + y0
            s_scr[p] = _dot(sb, m.astype(bf16)) + gm
            y_scr[rows, ls] = y[:chunk] + y[chunk:]
        return carry

    lax.fori_loop(0, nch, chunk_step, 0)

    bd = bd_ref[...]
    y = y_scr[...]
    inv_n = 1.0 / HEAD_DIM
    mean = _group_sum(y, bd) * inv_n
    d = y - mean
    var = _group_sum(d * d, bd) * inv_n
    yn = d * lax.rsqrt(var + GN_EPS) * gnw_ref[...] + gnb_ref[...]
    r = r_ref[...]
    bonus = _group_sum(r * k_ref[...] * rk_ref[...], bd) * v_ref[...]
    o_ref[...] = (yn + bonus) * g_ref[...]

    @pl.when(i == pl.num_programs(1) - 1)
    def _():
        st_ref[0] = s_scr[...]


def _rwkv_scan(vecs, s0_bd, rk, gnw, gnb, bd, b, t, tt, chunk):
    nt = t // tt
    n_pairs = D_RWKV // LANES
    row = pl.BlockSpec((tt, D_RWKV), lambda bi, i: (bi * nt + i, 0))
    stspec = pl.BlockSpec((1, n_pairs, LANES, LANES), lambda bi, i: (bi, 0, 0, 0))
    return pl.pallas_call(
        functools.partial(_rwkv_scan_kernel, chunk=chunk),
        grid=(b, nt),
        in_specs=[row] * 7 + [stspec, _const_spec(rk.shape), _const_spec(gnw.shape),
                              _const_spec(gnb.shape), _const_spec(bd.shape)],
        out_specs=[row, stspec],
        out_shape=[jax.ShapeDtypeStruct((b * t, D_RWKV), f32),
                   jax.ShapeDtypeStruct((b, n_pairs, LANES, LANES), f32)],
        scratch_shapes=[pltpu.VMEM((n_pairs, LANES, LANES), f32), pltpu.VMEM((tt, D_RWKV), f32)],
        compiler_params=pltpu.CompilerParams(
            dimension_semantics=("arbitrary", "arbitrary"), vmem_limit_bytes=VMEM_LIMIT),
        name="rwkv_scan",
    )(*vecs, s0_bd, rk, gnw, gnb, bd)


def _ffn_kernel(x_ref, oa_ref, orw_ref, woa_ref, wor_ref, gpm_ref, gpf_ref, wz_ref, wu_ref,
                cp_ref, wd_ref, gpo_ref, *rest, seq):
    if seq is None:
        zinit_ref, out_ref, ztail_ref, x1_scr, hn_scr, f_scr, zprev_scr = rest
    else:
        ca_ref, cb_ref, out_ref, ztail_ref, x1_scr, hn_scr, f_scr = rest
    tm = x_ref.shape[0]
    mix = _dot(oa_ref[...].astype(bf16), woa_ref[...]) + _dot(orw_ref[...].astype(bf16), wor_ref[...])
    x1 = x_ref[...] + _rms(mix, gpm_ref[...])
    x1_scr[...] = x1
    hn_scr[...] = _rms(x1, gpf_ref[...]).astype(bf16)
    f_scr[...] = jnp.zeros_like(f_scr)
    if seq is None:
        @pl.when(pl.program_id(1) == 0)
        def _():
            zprev_scr[...] = zinit_ref[0]
    row = lax.broadcasted_iota(jnp.int32, (tm, FF_CHUNK), 0)
    tmod = row if seq is None else row % seq

    def ff_step(c, carry):
        hn = hn_scr[...]
        z = _dot(hn, wz_ref[c])
        u = _dot(hn, wu_ref[c])
        cp = cp_ref[c]
        if seq is None:
            tail = zprev_scr[c]
            back2 = tail[SUBLANES - 2:SUBLANES - 1, :]
            back1 = tail[SUBLANES - 1:SUBLANES, :]
            zprev_scr[c] = z[tm - SUBLANES:, :]
            ztail_ref[0, c] = z[tm - SUBLANES:, :]
        else:
            back2 = ca_ref[c]
            back1 = cb_ref[c]
            ztail_ref[c] = z
        z1 = jnp.where(tmod == 0, back1, pltpu.roll(z, 1, 0))
        z2 = jnp.where(tmod == 0, back2, jnp.where(tmod == 1, back1, pltpu.roll(z, 2, 0)))
        zc = cp[3:4, :] + cp[0:1, :] * z2
        zc = zc + cp[1:2, :] * z1
        zc = zc + cp[2:3, :] * z
        hid = zc * jax.nn.sigmoid(zc) * u
        f_scr[...] += _dot(hid.astype(bf16), wd_ref[c])
        return carry

    lax.fori_loop(0, N_FF_CHUNKS, ff_step, 0)
    out_ref[...] = x1_scr[...] + _rms(f_scr[...], gpo_ref[...])


def _ffn(x2d, oa, orw, consts, extra, b, t, tm, seq):
    woa, wor, gpm, gpf, wz, wu, cp, wd, gpo = consts
    n = b * t
    if seq is None:
        nt = t // tm
        grid = (b, nt)
        row = lambda w: pl.BlockSpec((tm, w), lambda bi, i: (bi * nt + i, 0))
        tail_block = (1, N_FF_CHUNKS, SUBLANES, FF_CHUNK)
        extra_specs = [pl.BlockSpec(tail_block, lambda bi, i: (bi, 0, 0, 0))]
        tail_spec = pl.BlockSpec(tail_block, lambda bi, i: (bi, 0, 0, 0))
        tail_shape = jax.ShapeDtypeStruct((b,) + tail_block[1:], f32)
        scratch = [pltpu.VMEM((N_FF_CHUNKS, SUBLANES, FF_CHUNK), f32)]
        sem = ("arbitrary", "arbitrary")
    else:
        grid = (n // tm,)
        row = lambda w: pl.BlockSpec((tm, w), lambda i: (i, 0))
        cand = pl.BlockSpec((N_FF_CHUNKS, tm, FF_CHUNK), lambda i: (0, i, 0))
        extra_specs = [cand, cand]
        tail_spec = cand
        tail_shape = jax.ShapeDtypeStruct((N_FF_CHUNKS, n, FF_CHUNK), f32)
        scratch = []
        sem = ("arbitrary",)
    return pl.pallas_call(
        functools.partial(_ffn_kernel, seq=seq),
        grid=grid,
        in_specs=[row(D_MODEL), row(D_ATTN), row(D_RWKV)]
        + [_const_spec(c.shape) for c in consts] + extra_specs,
        out_specs=[row(D_MODEL), tail_spec],
        out_shape=[jax.ShapeDtypeStruct((n, D_MODEL), f32), tail_shape],
        scratch_shapes=[pltpu.VMEM((tm, D_MODEL), f32), pltpu.VMEM((tm, D_MODEL), bf16),
                        pltpu.VMEM((tm, D_MODEL), f32)] + scratch,
        compiler_params=pltpu.CompilerParams(dimension_semantics=sem, vmem_limit_bytes=VMEM_LIMIT),
        name="ffn_prompt" if seq is None else "ffn_sample",
    )(x2d, oa, orw, *consts, *extra)


def _rope_tables(pos):
    half = HEAD_DIM // 2
    inv = ROPE_THETA ** (-jnp.arange(half, dtype=f32) / half)
    ang = pos.astype(f32)[:, None] * inv[None, :]
    cos = jnp.cos(ang)
    sin = jnp.sin(ang)
    reps = LANES // HEAD_DIM
    cos_t = jnp.tile(jnp.concatenate([cos, cos], axis=-1), (1, reps))
    sin_t = jnp.tile(jnp.concatenate([-sin, sin], axis=-1), (1, reps))
    return cos_t, sin_t


def _pad_cols(x, width):
    return jnp.pad(x, ((0, 0),) * (x.ndim - 1) + ((0, width - x.shape[-1]),))


def _to_pairs(s):
    b = s.shape[0]
    s = s.reshape(b, RWKV_HEADS // 2, 2, HEAD_DIM, HEAD_DIM)
    z = jnp.zeros_like(s[:, :, 0])
    top = jnp.concatenate([s[:, :, 0], z], axis=-1)
    bot = jnp.concatenate([z, s[:, :, 1]], axis=-1)
    return jnp.concatenate([top, bot], axis=-2)


def _from_pairs(s):
    a = s[:, :, :HEAD_DIM, :HEAD_DIM]
    c = s[:, :, HEAD_DIM:, HEAD_DIM:]
    return jnp.stack([a, c], axis=2).reshape(s.shape[0], RWKV_HEADS, HEAD_DIM, HEAD_DIM)


def kernel(x_prompt, x_sample, cache_k_win, cache_v_win, state_shift, state_wkv, state_conv,
           g_pre_mix, w_in, attn_sinks, mu_shift, w0, w_decay_up, a0, w_a_up, w_g_up,
           k_k, k_a, r_k, gn_w, gn_b, w_out, g_post_mix, g_pre_ffn, w_ffn_in, conv_w,
           conv_b, w_ffn_out, g_post_ffn):
    B, T, _ = x_prompt.shape
    Bd, L, _ = x_sample.shape
    l = 0
    win = w_in[l].astype(bf16)
    wqkv = win[:, :D_QKV]
    wrkv = win[:, D_QKV:D_QKV + 3 * D_RWKV]
    wlora = _pad_cols(win[:, D_QKV + 3 * D_RWKV:], D_LORA_PAD)
    g_pm = g_pre_mix[l][None]
    mu = mu_shift[l]
    mu_rkv = mu[None, :3 * D_RWKV]
    mu_lora = _pad_cols(mu[None, 3 * D_RWKV:], D_LORA_PAD)
    wl = jnp.zeros((D_LORA_PAD, 3 * D_RWKV), f32)
    wl = wl.at[:D_DECAY_LORA, :D_RWKV].set(w_decay_up[l])
    wl = wl.at[D_DECAY_LORA:D_DECAY_LORA + D_A_LORA, D_RWKV:2 * D_RWKV].set(w_a_up[l])
    wl = wl.at[D_DECAY_LORA + D_A_LORA:D_LORA, 2 * D_RWKV:].set(w_g_up[l])
    wl = wl.astype(bf16)
    hid = jnp.arange(D_RWKV) // HEAD_DIM
    bd = (hid[:, None] == hid[None, :]).astype(bf16)
    prep_consts = (mu_rkv, mu_lora, wl, w0[l][None], a0[l][None], k_k[l][None], k_a[l][None], bd)
    rk = r_k[l].reshape(1, D_RWKV)
    gnw = gn_w[l][None]
    gnb = gn_b[l][None]
    wo = w_out[l].astype(bf16)
    wfi = w_ffn_in[l].astype(bf16)
    chunked = lambda w: w.reshape(D_MODEL, N_FF_CHUNKS, FF_CHUNK).transpose(1, 0, 2)
    wz = chunked(wfi[:, :D_FF])
    wu = chunked(wfi[:, D_FF:])
    wd = w_ffn_out[l].astype(bf16).reshape(N_FF_CHUNKS, FF_CHUNK, D_MODEL)
    cp = jnp.concatenate([conv_w[l], conv_b[l][None],
                          jnp.zeros((SUBLANES - CONV_W - 1, D_FF), f32)], axis=0)
    cp = cp.reshape(SUBLANES, N_FF_CHUNKS, FF_CHUNK).transpose(1, 0, 2)
    ffn_consts = (wo[:D_ATTN], wo[D_ATTN:], g_post_mix[l][None], g_pre_ffn[l][None],
                  wz, wu, cp, wd, g_post_ffn[l][None])
    sinks = attn_sinks[l]

    xp = x_prompt.reshape(B * T, D_MODEL)
    cos_p, sin_p = _rope_tables(jnp.arange(T, dtype=jnp.int32))
    qkv_p, rkv_p, lora_p = _in_proj(xp, g_pm, wqkv, wrkv, wlora, cos_p, sin_p, tm=512)
    oa_p = _attn_prompt(sinks, qkv_p.reshape(B, T, D_QKV), blk=WINDOW).reshape(B * T, D_ATTN)
    zs_rkv = jnp.zeros((B, 1, 3 * D_RWKV), f32)
    zs_lora = jnp.zeros((B, 1, D_LORA_PAD), f32)
    vecs_p = _rwkv_prep_prompt(rkv_p, lora_p, zs_rkv, zs_lora, prep_consts, B, T, tm=512)
    s0_p = jnp.zeros((B, RWKV_HEADS // 2, LANES, LANES), f32)
    orw_p, st_p = _rwkv_scan(vecs_p, s0_p, rk, gnw, gnb, bd, B, T, tt=512, chunk=RWKV_CHUNK)
    zinit = jnp.zeros((B, N_FF_CHUNKS, SUBLANES, FF_CHUNK), f32)
    y_p, ztail_p = _ffn(xp, oa_p, orw_p, ffn_consts, (zinit,), B, T, tm=512, seq=None)

    qkv_p3 = qkv_p.reshape(B, T, D_QKV)
    new_k_p = qkv_p3[:, T - WINDOW:, D_ATTN:D_ATTN + D_KV].reshape(B, WINDOW, N_KV_HEADS, HEAD_DIM)
    new_v_p = qkv_p3[:, T - WINDOW:, D_ATTN + D_KV:].reshape(B, WINDOW, N_KV_HEADS, HEAD_DIM)
    new_shift_p = jnp.concatenate(
        [rkv_p.reshape(B, T, -1)[:, -1], lora_p.reshape(B, T, -1)[:, -1, :D_LORA]], axis=-1)
    new_wkv_p = _from_pairs(st_p)
    new_conv_p = ztail_p[:, :, SUBLANES - (CONV_W - 1):, :].transpose(0, 2, 1, 3).reshape(
        B, CONV_W - 1, D_FF)

    ns = Bd * L
    xs = x_sample.reshape(ns, D_MODEL)
    cos_s, sin_s = _rope_tables(PAST_LEN + jnp.arange(L, dtype=jnp.int32))
    cos_s = jnp.tile(cos_s, (Bd, 1))
    sin_s = jnp.tile(sin_s, (Bd, 1))
    qkv_s, rkv_s, lora_s = _in_proj(xs, g_pm, wqkv, wrkv, wlora, cos_s, sin_s, tm=ns)
    ck = cache_k_win[l].reshape(Bd, WINDOW, D_KV)
    cv = cache_v_win[l].reshape(Bd, WINDOW, D_KV)
    oa_s, nk_s, nv_s = _attn_sample(sinks, qkv_s.reshape(Bd, L, D_QKV), ck, cv, bb=8)
    oa_s = oa_s.reshape(ns, D_ATTN)
    sh = state_shift[l]
    c_rkv = jnp.repeat(sh[:, :3 * D_RWKV], L, axis=0)
    c_lora = jnp.repeat(_pad_cols(sh[:, 3 * D_RWKV:], D_LORA_PAD), L, axis=0)
    vecs_s = _rwkv_prep_sample(rkv_s, lora_s, c_rkv, c_lora, prep_consts, seq=L, tm=ns)
    padt = lambda a: jnp.pad(a.reshape(Bd, L, D_RWKV), ((0, 0), (0, RWKV_CHUNK - L), (0, 0))
                             ).reshape(Bd * RWKV_CHUNK, D_RWKV)
    vecs_s_pad = tuple(padt(a) for a in vecs_s)
    orw_s, st_s = _rwkv_scan(vecs_s_pad, _to_pairs(state_wkv[l]), rk, gnw, gnb, bd,
                             Bd, RWKV_CHUNK, tt=RWKV_CHUNK, chunk=RWKV_CHUNK)
    orw_s = orw_s.reshape(Bd, RWKV_CHUNK, D_RWKV)[:, :L].reshape(ns, D_RWKV)
    sc = state_conv[l]
    cand = lambda a: jnp.repeat(a, L, axis=0).reshape(ns, N_FF_CHUNKS, FF_CHUNK).transpose(1, 0, 2)
    y_s, z_s = _ffn(xs, oa_s, orw_s, ffn_consts, (cand(sc[:, 0]), cand(sc[:, 1])),
                    Bd, L, tm=256, seq=L)
    new_shift_s = jnp.concatenate(
        [rkv_s.reshape(Bd, L, -1)[:, -1], lora_s.reshape(Bd, L, -1)[:, -1, :D_LORA]], axis=-1)
    new_conv_s = z_s.transpose(1, 0, 2).reshape(Bd, L, D_FF)[:, L - (CONV_W - 1):]

    lead = lambda a: a[None]
    return (y_p.reshape(B, T, D_MODEL), y_s.reshape(Bd, L, D_MODEL),
            lead(new_k_p), lead(new_v_p), lead(new_shift_p), lead(new_wkv_p), lead(new_conv_p),
            lead(nk_s.reshape(Bd, WINDOW, N_KV_HEADS, HEAD_DIM)),
            lead(nv_s.reshape(Bd, WINDOW, N_KV_HEADS, HEAD_DIM)),
            lead(new_shift_s), lead(_from_pairs(st_s)), lead(new_conv_s))
```

```python
import functools
import math

import jax
import jax.numpy as jnp
from jax import lax
from jax.experimental import pallas as pl
from jax.experimental.pallas import tpu as pltpu

f32 = jnp.float32
bf16 = jnp.bfloat16

D_MODEL = 1024
HEAD_DIM = 64
D_ATTN = 512
D_RWKV = 512
N_HEADS = 8
N_KV_HEADS = 2
GQA_GROUP = 4
D_KV = 128
WINDOW = 128
ROPE_THETA = 10000.0
RWKV_HEADS = 8
D_DECAY_LORA = 32
D_A_LORA = 32
D_GATE_LORA = 96
D_LORA = D_DECAY_LORA + D_A_LORA + D_GATE_LORA
D_LORA_PAD = 256
D_SHIFT = 3 * D_RWKV + D_LORA
D_QKV = D_ATTN + 2 * D_KV
D_FF = 2816
FF_CHUNK = 256
N_FF_CHUNKS = D_FF // FF_CHUNK
CONV_W = 3
PAST_LEN = 16384
RMS_EPS = 1e-6
GN_EPS = 64e-5
NEG_INF = -1e30
LANES = 128
SUBLANES = 8
RWKV_CHUNK = 64
VMEM_LIMIT = 56 * 1024 * 1024


def _const_spec(shape):
    nd = len(shape)
    return pl.BlockSpec(shape, lambda *_: (0,) * nd, pipeline_mode=pl.Buffered(1))


def _rms(x, g):
    ms = jnp.mean(x * x, axis=-1, keepdims=True)
    return x * lax.rsqrt(ms + RMS_EPS) * g


def _dot(a, b):
    return jnp.dot(a, b, preferred_element_type=f32)


def _dot_nt(a, b):
    return lax.dot_general(a, b, (((1,), (1,)), ((), ())), preferred_element_type=f32)


def _dot_tn(a, b):
    return lax.dot_general(a, b, (((0,), (0,)), ((), ())), preferred_element_type=f32)


def _group_sum(x, bd):
    hi = x.astype(bf16)
    r1 = x - hi.astype(f32)
    mid = r1.astype(bf16)
    lo = (r1 - mid.astype(f32)).astype(bf16)
    return _dot(hi, bd) + _dot(mid, bd) + _dot(lo, bd)


def _in_proj_kernel(x_ref, g_ref, wqkv_ref, wrkv_ref, wlora_ref, cos_ref, sin_ref,
                    qkv_ref, rkv_ref, lora_ref):
    hn = _rms(x_ref[...], g_ref[...]).astype(bf16)
    qkv = _dot(hn, wqkv_ref[...])
    cos = cos_ref[...]
    sin = sin_ref[...]
    lane = lax.broadcasted_iota(jnp.int32, cos.shape, 1)
    first_half = (lane % HEAD_DIM) < (HEAD_DIM // 2)
    for j in range((D_ATTN + D_KV) // LANES):
        xs = qkv[:, j * LANES:(j + 1) * LANES]
        partner = jnp.where(first_half,
                            pltpu.roll(xs, LANES - HEAD_DIM // 2, 1),
                            pltpu.roll(xs, HEAD_DIM // 2, 1))
        qkv_ref[:, j * LANES:(j + 1) * LANES] = xs * cos + partner * sin
    qkv_ref[:, D_ATTN + D_KV:] = qkv[:, D_ATTN + D_KV:]
    rkv_ref[...] = _dot(hn, wrkv_ref[...])
    lora_ref[...] = _dot(hn, wlora_ref[...])


def _in_proj(x2d, g, wqkv, wrkv, wlora, cos, sin, tm):
    n = x2d.shape[0]
    n_pos = cos.shape[0] // tm
    row = lambda w: pl.BlockSpec((tm, w), lambda i: (i, 0))
    pos = pl.BlockSpec((tm, LANES), lambda i: (i % n_pos, 0))
    return pl.pallas_call(
        _in_proj_kernel,
        grid=(n // tm,),
        in_specs=[row(D_MODEL), _const_spec((1, D_MODEL)), _const_spec(wqkv.shape),
                  _const_spec(wrkv.shape), _const_spec(wlora.shape), pos, pos],
        out_specs=[row(D_QKV), row(3 * D_RWKV), row(D_LORA_PAD)],
        out_shape=[jax.ShapeDtypeStruct((n, D_QKV), f32),
                   jax.ShapeDtypeStruct((n, 3 * D_RWKV), f32),
                   jax.ShapeDtypeStruct((n, D_LORA_PAD), f32)],
        compiler_params=pltpu.CompilerParams(
            dimension_semantics=("arbitrary",), vmem_limit_bytes=VMEM_LIMIT),
        name="in_proj",
    )(x2d, g, wqkv, wrkv, wlora, cos, sin)


def _softmax_sink(s, sink_col):
    m = jnp.maximum(jnp.max(s, axis=-1, keepdims=True), sink_col)
    p = jnp.exp(s - m)
    den = jnp.sum(p, axis=-1, keepdims=True) + jnp.exp(sink_col - m)
    return p / den


def _sink_column(sink_ref, kh, rows_per_head, shape):
    row = lax.broadcasted_iota(jnp.int32, shape, 0)
    col = jnp.full(shape, sink_ref[kh * GQA_GROUP + GQA_GROUP - 1], f32)
    for g in range(GQA_GROUP - 2, -1, -1):
        col = jnp.where(row < (g + 1) * rows_per_head, sink_ref[kh * GQA_GROUP + g], col)
    return col


def _attn_prompt_kernel(sink_ref, q_ref, kvc_ref, kvp_ref, o_ref):
    blk = q_ref.shape[1]
    i = pl.program_id(1)
    q = q_ref[0] * (HEAD_DIM ** -0.5)
    kvc = kvc_ref[0]
    kvp = kvp_ref[0]
    shape = (GQA_GROUP * blk, 2 * blk)
    qi = lax.broadcasted_iota(jnp.int32, shape, 0) % blk
    sj = lax.broadcasted_iota(jnp.int32, shape, 1)
    dist = qi + blk - sj
    mask = (dist >= 0) & (dist <= WINDOW) & ((sj >= blk) | (i > 0))
    for kh in range(N_KV_HEADS):
        ks = slice(kh * HEAD_DIM, (kh + 1) * HEAD_DIM)
        vs = slice(D_KV + kh * HEAD_DIM, D_KV + (kh + 1) * HEAD_DIM)
        kcat = jnp.concatenate([kvp[:, ks], kvc[:, ks]], axis=0).astype(bf16)
        vcat = jnp.concatenate([kvp[:, vs], kvc[:, vs]], axis=0).astype(bf16)
        qs = jnp.concatenate(
            [q[:, (kh * GQA_GROUP + g) * HEAD_DIM:(kh * GQA_GROUP + g + 1) * HEAD_DIM]
             for g in range(GQA_GROUP)], axis=0).astype(bf16)
        s = jnp.where(mask, _dot_nt(qs, kcat), NEG_INF)
        p = _softmax_sink(s, _sink_column(sink_ref, kh, blk, (GQA_GROUP * blk, 1)))
        o = _dot(p.astype(bf16), vcat)
        for g in range(GQA_GROUP):
            h = kh * GQA_GROUP + g
            o_ref[0, :, h * HEAD_DIM:(h + 1) * HEAD_DIM] = o[g * blk:(g + 1) * blk]


def _attn_prompt(sinks, qkv3, blk):
    b, t, _ = qkv3.shape
    return pl.pallas_call(
        _attn_prompt_kernel,
        grid=(b, t // blk),
        in_specs=[pl.BlockSpec(memory_space=pltpu.SMEM),
                  pl.BlockSpec((1, blk, D_ATTN), lambda bi, i: (bi, i, 0)),
                  pl.BlockSpec((1, blk, 2 * D_KV), lambda bi, i: (bi, i, D_ATTN // (2 * D_KV))),
                  pl.BlockSpec((1, blk, 2 * D_KV),
                               lambda bi, i: (bi, jnp.maximum(i - 1, 0), D_ATTN // (2 * D_KV)))],
        out_specs=pl.BlockSpec((1, blk, D_ATTN), lambda bi, i: (bi, i, 0)),
        out_shape=jax.ShapeDtypeStruct((b, t, D_ATTN), f32),
        compiler_params=pltpu.CompilerParams(
            dimension_semantics=("arbitrary", "arbitrary"), vmem_limit_bytes=VMEM_LIMIT),
        name="attn_prompt",
    )(sinks, qkv3, qkv3, qkv3)


def _attn_sample_kernel(sink_ref, qkv_ref, ck_ref, cv_ref, o_ref, nk_ref, nv_ref):
    bb, L, _ = qkv_ref.shape
    qkv = qkv_ref[...]
    q = qkv[:, :, :D_ATTN] * (HEAD_DIM ** -0.5)
    kn = qkv[:, :, D_ATTN:D_ATTN + D_KV]
    vn = qkv[:, :, D_ATTN + D_KV:]
    ck = ck_ref[...]
    cv = cv_ref[...]
    nk_ref[:, :WINDOW - L, :] = ck[:, L:, :]
    nk_ref[:, WINDOW - L:, :] = kn
    nv_ref[:, :WINDOW - L, :] = cv[:, L:, :]
    nv_ref[:, WINDOW - L:, :] = vn
    rows = GQA_GROUP * L
    qi = lax.broadcasted_iota(jnp.int32, (rows, WINDOW), 0) % L
    sj = lax.broadcasted_iota(jnp.int32, (rows, WINDOW), 1)
    mask_c = (sj >= qi)[None]
    qn = lax.broadcasted_iota(jnp.int32, (rows, L), 0) % L
    jn = lax.broadcasted_iota(jnp.int32, (rows, L), 1)
    mask_n = (jn <= qn)[None]
    for kh in range(N_KV_HEADS):
        ks = slice(kh * HEAD_DIM, (kh + 1) * HEAD_DIM)
        qs = jnp.concatenate(
            [q[:, :, (kh * GQA_GROUP + g) * HEAD_DIM:(kh * GQA_GROUP + g + 1) * HEAD_DIM]
             for g in range(GQA_GROUP)], axis=1)
        qsb = qs.astype(bf16)
        s_c = jnp.einsum('bqd,bkd->bqk', qsb, ck[:, :, ks].astype(bf16),
                         preferred_element_type=f32)
        s_n = jnp.einsum('bqd,bkd->bqk', qsb.astype(f32), kn[:, :, ks].astype(bf16).astype(f32),
                         preferred_element_type=f32, precision=lax.Precision.HIGHEST)
        s_c = jnp.where(mask_c, s_c, NEG_INF)
        s_n = jnp.where(mask_n, s_n, NEG_INF)
        sink_col = _sink_column(sink_ref, kh, L, (rows, 1))[None]
        m = jnp.maximum(jnp.maximum(jnp.max(s_c, axis=-1, keepdims=True),
                                    jnp.max(s_n, axis=-1, keepdims=True)), sink_col)
        p_c = jnp.exp(s_c - m)
        p_n = jnp.exp(s_n - m)
        den = (jnp.sum(p_c, axis=-1, keepdims=True) + jnp.sum(p_n, axis=-1, keepdims=True)
               + jnp.exp(sink_col - m))
        p_c = (p_c / den).astype(bf16)
        p_n = (p_n / den).astype(bf16).astype(f32)
        o = jnp.einsum('bqk,bkd->bqd', p_c, cv[:, :, ks].astype(bf16), preferred_element_type=f32)
        o = o + jnp.einsum('bqk,bkd->bqd', p_n, vn[:, :, ks].astype(bf16).astype(f32),
                           preferred_element_type=f32, precision=lax.Precision.HIGHEST)
        for g in range(GQA_GROUP):
            h = kh * GQA_GROUP + g
            o_ref[:, :, h * HEAD_DIM:(h + 1) * HEAD_DIM] = o[:, g * L:(g + 1) * L, :]


def _attn_sample(sinks, qkv3, ck, cv, bb):
    b, L, _ = qkv3.shape
    blk3 = lambda r, w: pl.BlockSpec((bb, r, w), lambda i: (i, 0, 0))
    return pl.pallas_call(
        _attn_sample_kernel,
        grid=(b // bb,),
        in_specs=[pl.BlockSpec(memory_space=pltpu.SMEM), blk3(L, D_QKV),
                  blk3(WINDOW, D_KV), blk3(WINDOW, D_KV)],
        out_specs=[blk3(L, D_ATTN), blk3(WINDOW, D_KV), blk3(WINDOW, D_KV)],
        out_shape=[jax.ShapeDtypeStruct((b, L, D_ATTN), f32),
                   jax.ShapeDtypeStruct((b, WINDOW, D_KV), f32),
                   jax.ShapeDtypeStruct((b, WINDOW, D_KV), f32)],
        compiler_params=pltpu.CompilerParams(
            dimension_semantics=("arbitrary",), vmem_limit_bytes=VMEM_LIMIT),
        name="attn_sample",
    )(sinks, qkv3, ck, cv)


def _rwkv_prep_body(h_rkv, h_lora, p_rkv, p_lora, mu_rkv, mu_lora, wl, w0, a0, kk_w, ka_w, bd,
                    r_ref, lw_ref, k_ref, v_ref, kk_ref, ba_ref, g_ref):
    hs = h_rkv + (p_rkv - h_rkv) * mu_rkv
    ls = h_lora + (p_lora - h_lora) * mu_lora
    r = hs[:, :D_RWKV]
    k = hs[:, D_RWKV:2 * D_RWKV]
    v = hs[:, 2 * D_RWKV:]
    lane = lax.broadcasted_iota(jnp.int32, ls.shape, 1)
    act = jnp.where(lane < D_DECAY_LORA, jnp.tanh(ls),
                    jnp.where(lane < D_DECAY_LORA + D_A_LORA, ls, jax.nn.sigmoid(ls)))
    pre = _dot(act.astype(bf16), wl)
    xw = -(w0 + pre[:, :D_RWKV])
    softplus = jnp.maximum(xw, 0.0) + jnp.log1p(jnp.exp(-jnp.abs(xw)))
    w = -softplus - 0.5
    lw_ref[...] = -jnp.exp(w)
    a = jax.nn.sigmoid(a0 + pre[:, D_RWKV:2 * D_RWKV])
    g_ref[...] = pre[:, 2 * D_RWKV:]
    kkp = k * kk_w
    kk = kkp * lax.rsqrt(jnp.maximum(_group_sum(kkp * kkp, bd), 1e-24))
    r_ref[...] = r
    k_ref[...] = k * (1.0 + (a - 1.0) * ka_w)
    v_ref[...] = v
    kk_ref[...] = kk
    ba_ref[...] = kk * a


def _shift_rows(x, first_row):
    row = lax.broadcasted_iota(jnp.int32, x.shape, 0)
    return jnp.where(row == 0, first_row, pltpu.roll(x, 1, 0))


def _rwkv_prep_prompt_kernel(rkv_ref, lora_ref, prkv_ref, plora_ref, s_rkv_ref, s_lora_ref,
                             mu_rkv_ref, mu_lora_ref, wl_ref, w0_ref, a0_ref, kkw_ref, kaw_ref,
                             bd_ref, *out_refs):
    i = pl.program_id(1)
    h_rkv = rkv_ref[...]
    h_lora = lora_ref[...]
    first = i == 0
    prev_rkv = jnp.where(first, s_rkv_ref[0], prkv_ref[SUBLANES - 1:SUBLANES, :])
    prev_lora = jnp.where(first, s_lora_ref[0], plora_ref[SUBLANES - 1:SUBLANES, :])
    _rwkv_prep_body(h_rkv, h_lora, _shift_rows(h_rkv, prev_rkv), _shift_rows(h_lora, prev_lora),
                    mu_rkv_ref[...], mu_lora_ref[...], wl_ref[...], w0_ref[...], a0_ref[...],
                    kkw_ref[...], kaw_ref[...], bd_ref[...], *out_refs)


def _rwkv_prep_sample_kernel(rkv_ref, lora_ref, c_rkv_ref, c_lora_ref, mu_rkv_ref, mu_lora_ref,
                             wl_ref, w0_ref, a0_ref, kkw_ref, kaw_ref, bd_ref, *out_refs, seq):
    h_rkv = rkv_ref[...]
    h_lora = lora_ref[...]

    def prev(h, cand):
        t = lax.broadcasted_iota(jnp.int32, h.shape, 0) % seq
        return jnp.where(t == 0, cand, pltpu.roll(h, 1, 0))

    _rwkv_prep_body(h_rkv, h_lora, prev(h_rkv, c_rkv_ref[...]), prev(h_lora, c_lora_ref[...]),
                    mu_rkv_ref[...], mu_lora_ref[...], wl_ref[...], w0_ref[...], a0_ref[...],
                    kkw_ref[...], kaw_ref[...], bd_ref[...], *out_refs)


def _rwkv_prep_prompt(rkv, lora, s_rkv, s_lora, consts, b, t, tm):
    nt = t // tm
    per8 = tm // SUBLANES
    row = lambda w: pl.BlockSpec((tm, w), lambda bi, i: (bi * nt + i, 0))
    prow = lambda w: pl.BlockSpec(
        (SUBLANES, w), lambda bi, i: (jnp.maximum((bi * nt + i) * per8 - 1, 0), 0))
    st = lambda w: pl.BlockSpec((1, 1, w), lambda bi, i: (bi, 0, 0))
    n = b * t
    return pl.pallas_call(
        _rwkv_prep_prompt_kernel,
        grid=(b, nt),
        in_specs=[row(3 * D_RWKV), row(D_LORA_PAD), prow(3 * D_RWKV), prow(D_LORA_PAD),
                  st(3 * D_RWKV), st(D_LORA_PAD)] + [_const_spec(c.shape) for c in consts],
        out_specs=[row(D_RWKV)] * 7,
        out_shape=[jax.ShapeDtypeStruct((n, D_RWKV), f32)] * 7,
        compiler_params=pltpu.CompilerParams(
            dimension_semantics=("arbitrary", "arbitrary"), vmem_limit_bytes=VMEM_LIMIT),
        name="rwkv_prep_prompt",
    )(rkv, lora, rkv, lora, s_rkv, s_lora, *consts)


def _rwkv_prep_sample(rkv, lora, c_rkv, c_lora, consts, seq, tm):
    n = rkv.shape[0]
    row = lambda w: pl.BlockSpec((tm, w), lambda i: (i, 0))
    return pl.pallas_call(
        functools.partial(_rwkv_prep_sample_kernel, seq=seq),
        grid=(n // tm,),
        in_specs=[row(3 * D_RWKV), row(D_LORA_PAD), row(3 * D_RWKV), row(D_LORA_PAD)]
        + [_const_spec(c.shape) for c in consts],
        out_specs=[row(D_RWKV)] * 7,
        out_shape=[jax.ShapeDtypeStruct((n, D_RWKV), f32)] * 7,
        compiler_params=pltpu.CompilerParams(
            dimension_semantics=("arbitrary",), vmem_limit_bytes=VMEM_LIMIT),
        name="rwkv_prep_sample",
    )(rkv, lora, c_rkv, c_lora, *consts)


def _interleave(gens):
    active = list(gens)
    while active:
        still = []
        for gen in active:
            try:
                next(gen)
                still.append(gen)
            except StopIteration:
                pass
        active = still


def _rwkv_scan_kernel(r_ref, lw_ref, k_ref, v_ref, kk_ref, ba_ref, g_ref, s0_ref,
                      rk_ref, gnw_ref, gnb_ref, bd_ref, o_ref, st_ref, *scratch, chunk, chained):
    tt = r_ref.shape[0]
    nch = tt // chunk
    c2 = 2 * chunk
    n_pairs = D_RWKV // LANES
    n_factors = int(math.log2(chunk))
    assert 2 ** n_factors == chunk and n_factors >= 3
    if chained:
        s_scr, y_scr = scratch

        @pl.when(pl.program_id(1) == 0)
        def _():
            s_scr[...] = s0_ref[0]
    else:
        (y_scr,) = scratch

    ti = lax.broadcasted_iota(jnp.int32, (chunk, chunk), 0)
    tj = lax.broadcasted_iota(jnp.int32, (chunk, chunk), 1)
    tri = (ti >= tj).astype(bf16)
    ri = lax.broadcasted_iota(jnp.int32, (c2, c2), 0)
    rj = lax.broadcasted_iota(jnp.int32, (c2, c2), 1)
    strict = (ri % chunk) > (rj % chunk)
    incl = (ri % chunk) >= (rj % chunk)
    eye2 = (ri == rj).astype(f32)
    li = lax.broadcasted_iota(jnp.int32, (LANES, LANES), 0)
    lj = lax.broadcasted_iota(jnp.int32, (LANES, LANES), 1)
    eye_l = li == lj
    low_half = lax.broadcasted_iota(jnp.int32, (chunk, LANES), 1) < HEAD_DIM

    def stack(x):
        return jnp.concatenate([jnp.where(low_half, x, 0.0), jnp.where(low_half, 0.0, x)], axis=0)

    prep = {}
    res = {}
    state = {p: s_scr[p] for p in range(n_pairs)} if chained else {}

    def prep_gen(c):
        rows = slice(c * chunk, (c + 1) * chunk)
        lw = lw_ref[rows, :]
        hi = lw.astype(bf16)
        r1 = lw - hi.astype(f32)
        mid = r1.astype(bf16)
        lo = (r1 - mid.astype(f32)).astype(bf16)
        cl = _dot(tri, hi) + _dot(tri, mid) + _dot(tri, lo)
        yield
        e_fwd = jnp.exp(cl)
        e_inv = jnp.exp(-cl)
        prep[c] = (r_ref[rows, :] * e_fwd, -kk_ref[rows, :] * jnp.exp(cl - lw),
                   ba_ref[rows, :] * e_inv, k_ref[rows, :] * e_inv, v_ref[rows, :],
                   e_fwd[chunk - 1:chunk, :])

    def inst_gen(c, p):
        rt_all, at_all, bt_all, kt_all, v_all, pc = prep[c]
        ls = slice(p * LANES, (p + 1) * LANES)
        rt_f = stack(rt_all[:, ls])
        rt = rt_f.astype(bf16)
        at = stack(at_all[:, ls]).astype(bf16)
        bt_f = stack(bt_all[:, ls])
        kt_f = stack(kt_all[:, ls])
        pcp = pc[:, ls]
        bh = (bt_f * pcp).astype(bf16)
        kh = (kt_f * pcp).astype(bf16)
        vs = stack(v_all[:, ls]).astype(bf16)
        aa = _dot_nt(jnp.concatenate([at, rt], axis=0),
                     jnp.concatenate([bt_f.astype(bf16), kt_f.astype(bf16)], axis=0))
        yield
        lab = jnp.where(strict, aa[:c2, :c2], 0.0)
        lak = jnp.where(strict, aa[:c2, c2:], 0.0).astype(bf16)
        arb = jnp.where(incl, aa[c2:, :c2], 0.0).astype(bf16)
        ark = jnp.where(incl, aa[c2:, c2:], 0.0).astype(bf16)
        lab_b = lab.astype(bf16)
        lp = _dot(lab_b, lab_b)
        x = _dot(lak, vs).astype(bf16)
        arkv = _dot(ark, vs)
        vtk = _dot_tn(vs, kh)
        tm = eye2 + lab
        yield
        for _ in range(n_factors - 2):
            lpb = lp.astype(bf16)
            prod = _dot(jnp.concatenate([tm.astype(bf16), lpb], axis=0), lpb)
            tm = tm + prod[:c2]
            lp = prod[c2:]
            yield
        tm = tm + _dot(tm.astype(bf16), lp.astype(bf16))
        yield
        wu = _dot(tm.astype(bf16), jnp.concatenate([at, x], axis=1)).astype(bf16)
        yield
        mg = _dot_tn(wu, bh)
        qy = _dot(arb, wu)
        m = jnp.where(eye_l, pcp, 0.0) + mg[:LANES]
        res[(c, p)] = ((rt_f + qy[:, :LANES]).astype(bf16), qy[:, LANES:] + arkv,
                       m.astype(bf16), mg[LANES:] + vtk)

    def chain_gen(chunks):
        for c in chunks:
            rows = slice(c * chunk, (c + 1) * chunk)
            for p in range(n_pairs):
                qb, y0, mb, gm = res.pop((c, p))
                sb = (state[p] if chained else s0_ref[c, p]).astype(bf16)
                y = _dot_nt(qb, sb) + y0
                s_new = _dot(sb, mb) + gm
                if chained:
                    state[p] = s_new
                else:
                    st_ref[c, p] = s_new
                y_scr[rows, p * LANES:(p + 1) * LANES] = y[:chunk] + y[chunk:]
            yield

    _interleave([prep_gen(c) for c in range(nch)])
    group = 2
    prev = None
    for g0 in range(0, nch, group):
        cur = list(range(g0, min(g0 + group, nch)))
        gens = [inst_gen(c, p) for c in cur for p in range(n_pairs)]
        if prev is not None:
            gens.append(chain_gen(prev))
        _interleave(gens)
        prev = cur
    _interleave([chain_gen(prev)])
    if chained:
        for p in range(n_pairs):
            s_scr[p] = state[p]

    bd = bd_ref[...]
    y = y_scr[...]
    inv_n = 1.0 / HEAD_DIM
    mean = _group_sum(y, bd) * inv_n
    d = y - mean
    var = _group_sum(d * d, bd) * inv_n
    yn = d * lax.rsqrt(var + GN_EPS) * gnw_ref[...] + gnb_ref[...]
    r = r_ref[...]
    bonus = _group_sum(r * k_ref[...] * rk_ref[...], bd) * v_ref[...]
    o_ref[...] = (yn + bonus) * g_ref[...]

    if chained:
        @pl.when(pl.program_id(1) == pl.num_programs(1) - 1)
        def _():
            st_ref[0] = s_scr[...]


def _rwkv_scan(vecs, s0_bd, rk, gnw, gnb, bd, b, t, tt, chunk):
    n_pairs = D_RWKV // LANES
    chained = t != chunk
    if chained:
        nt = t // tt
        grid = (b, nt)
        row = pl.BlockSpec((tt, D_RWKV), lambda bi, i: (bi * nt + i, 0))
        stspec = pl.BlockSpec((1, n_pairs, LANES, LANES), lambda bi, i: (bi, 0, 0, 0))
        scratch = [pltpu.VMEM((n_pairs, LANES, LANES), f32)]
        sem = ("arbitrary", "arbitrary")
    else:
        nseq = tt // chunk
        grid = (b // nseq,)
        row = pl.BlockSpec((tt, D_RWKV), lambda i: (i, 0))
        stspec = pl.BlockSpec((nseq, n_pairs, LANES, LANES), lambda i: (i, 0, 0, 0))
        scratch = []
        sem = ("arbitrary",)
    return pl.pallas_call(
        functools.partial(_rwkv_scan_kernel, chunk=chunk, chained=chained),
        grid=grid,
        in_specs=[row] * 7 + [stspec, _const_spec(rk.shape), _const_spec(gnw.shape),
                              _const_spec(gnb.shape), _const_spec(bd.shape)],
        out_specs=[row, stspec],
        out_shape=[jax.ShapeDtypeStruct((b * t, D_RWKV), f32),
                   jax.ShapeDtypeStruct((b, n_pairs, LANES, LANES), f32)],
        scratch_shapes=scratch + [pltpu.VMEM((tt, D_RWKV), f32)],
        compiler_params=pltpu.CompilerParams(dimension_semantics=sem, vmem_limit_bytes=VMEM_LIMIT),
        name="rwkv_scan",
    )(*vecs, s0_bd, rk, gnw, gnb, bd)


def _ffn_kernel(x_ref, oa_ref, orw_ref, woa_ref, wor_ref, gpm_ref, gpf_ref, wz_ref, wu_ref,
                cp_ref, wd_ref, gpo_ref, *rest, seq):
    if seq is None:
        zinit_ref, out_ref, ztail_ref, x1_scr, hn_scr, f_scr, zprev_scr = rest
    else:
        ca_ref, cb_ref, out_ref, ztail_ref, x1_scr, hn_scr, f_scr = rest
    tm = x_ref.shape[0]
    mix = _dot(oa_ref[...].astype(bf16), woa_ref[...]) + _dot(orw_ref[...].astype(bf16), wor_ref[...])
    x1 = x_ref[...] + _rms(mix, gpm_ref[...])
    x1_scr[...] = x1
    hn_scr[...] = _rms(x1, gpf_ref[...]).astype(bf16)
    f_scr[...] = jnp.zeros_like(f_scr)
    if seq is None:
        @pl.when(pl.program_id(1) == 0)
        def _():
            zprev_scr[...] = zinit_ref[0]
    row = lax.broadcasted_iota(jnp.int32, (tm, FF_CHUNK), 0)
    tmod = row if seq is None else row % seq

    def ff_step(c, carry):
        hn = hn_scr[...]
        z = _dot(hn, wz_ref[c])
        u = _dot(hn, wu_ref[c])
        cp = cp_ref[c]
        if seq is None:
            tail = zprev_scr[c]
            back2 = tail[SUBLANES - 2:SUBLANES - 1, :]
            back1 = tail[SUBLANES - 1:SUBLANES, :]
            zprev_scr[c] = z[tm - SUBLANES:, :]
            ztail_ref[0, c] = z[tm - SUBLANES:, :]
        else:
            back2 = ca_ref[c]
            back1 = cb_ref[c]
            ztail_ref[c] = z
        z1 = jnp.where(tmod == 0, back1, pltpu.roll(z, 1, 0))
        z2 = jnp.where(tmod == 0, back2, jnp.where(tmod == 1, back1, pltpu.roll(z, 2, 0)))
        zc = cp[3:4, :] + cp[0:1, :] * z2
        zc = zc + cp[1:2, :] * z1
        zc = zc + cp[2:3, :] * z
        hid = zc * jax.nn.sigmoid(zc) * u
        f_scr[...] += _dot(hid.astype(bf16), wd_ref[c])
        return carry

    lax.fori_loop(0, N_FF_CHUNKS, ff_step, 0)
    out_ref[...] = x1_scr[...] + _rms(f_scr[...], gpo_ref[...])


def _ffn(x2d, oa, orw, consts, extra, b, t, tm, seq):
    woa, wor, gpm, gpf, wz, wu, cp, wd, gpo = consts
    n = b * t
    if seq is None:
        nt = t // tm
        grid = (b, nt)
        row = lambda w: pl.BlockSpec((tm, w), lambda bi, i: (bi * nt + i, 0))
        tail_block = (1, N_FF_CHUNKS, SUBLANES, FF_CHUNK)
        extra_specs = [pl.BlockSpec(tail_block, lambda bi, i: (bi, 0, 0, 0))]
        tail_spec = pl.BlockSpec(tail_block, lambda bi, i: (bi, 0, 0, 0))
        tail_shape = jax.ShapeDtypeStruct((b,) + tail_block[1:], f32)
        scratch = [pltpu.VMEM((N_FF_CHUNKS, SUBLANES, FF_CHUNK), f32)]
        sem = ("arbitrary", "arbitrary")
    else:
        grid = (n // tm,)
        row = lambda w: pl.BlockSpec((tm, w), lambda i: (i, 0))
        cand = pl.BlockSpec((N_FF_CHUNKS, tm, FF_CHUNK), lambda i: (0, i, 0))
        extra_specs = [cand, cand]
        tail_spec = cand
        tail_shape = jax.ShapeDtypeStruct((N_FF_CHUNKS, n, FF_CHUNK), f32)
        scratch = []
        sem = ("arbitrary",)
    return pl.pallas_call(
        functools.partial(_ffn_kernel, seq=seq),
        grid=grid,
        in_specs=[row(D_MODEL), row(D_ATTN), row(D_RWKV)]
        + [_const_spec(c.shape) for c in consts] + extra_specs,
        out_specs=[row(D_MODEL), tail_spec],
        out_shape=[jax.ShapeDtypeStruct((n, D_MODEL), f32), tail_shape],
        scratch_shapes=[pltpu.VMEM((tm, D_MODEL), f32), pltpu.VMEM((tm, D_MODEL), bf16),
                        pltpu.VMEM((tm, D_MODEL), f32)] + scratch,
        compiler_params=pltpu.CompilerParams(dimension_semantics=sem, vmem_limit_bytes=VMEM_LIMIT),
        name="ffn_prompt" if seq is None else "ffn_sample",
    )(x2d, oa, orw, *consts, *extra)


def _rope_tables(pos):
    half = HEAD_DIM // 2
    inv = ROPE_THETA ** (-jnp.arange(half, dtype=f32) / half)
    ang = pos.astype(f32)[:, None] * inv[None, :]
    cos = jnp.cos(ang)
    sin = jnp.sin(ang)
    reps = LANES // HEAD_DIM
    cos_t = jnp.tile(jnp.concatenate([cos, cos], axis=-1), (1, reps))
    sin_t = jnp.tile(jnp.concatenate([-sin, sin], axis=-1), (1, reps))
    return cos_t, sin_t


def _pad_cols(x, width):
    return jnp.pad(x, ((0, 0),) * (x.ndim - 1) + ((0, width - x.shape[-1]),))


def _to_pairs(s):
    b = s.shape[0]
    s = s.reshape(b, RWKV_HEADS // 2, 2, HEAD_DIM, HEAD_DIM)
    z = jnp.zeros_like(s[:, :, 0])
    top = jnp.concatenate([s[:, :, 0], z], axis=-1)
    bot = jnp.concatenate([z, s[:, :, 1]], axis=-1)
    return jnp.concatenate([top, bot], axis=-2)


def _from_pairs(s):
    a = s[:, :, :HEAD_DIM, :HEAD_DIM]
    c = s[:, :, HEAD_DIM:, HEAD_DIM:]
    return jnp.stack([a, c], axis=2).reshape(s.shape[0], RWKV_HEADS, HEAD_DIM, HEAD_DIM)


def kernel(x_prompt, x_sample, cache_k_win, cache_v_win, state_shift, state_wkv, state_conv,
           g_pre_mix, w_in, attn_sinks, mu_shift, w0, w_decay_up, a0, w_a_up, w_g_up,
           k_k, k_a, r_k, gn_w, gn_b, w_out, g_post_mix, g_pre_ffn, w_ffn_in, conv_w,
           conv_b, w_ffn_out, g_post_ffn):
    B, T, _ = x_prompt.shape
    Bd, L, _ = x_sample.shape
    l = 0
    win = w_in[l].astype(bf16)
    wqkv = win[:, :D_QKV]
    wrkv = win[:, D_QKV:D_QKV + 3 * D_RWKV]
    wlora = _pad_cols(win[:, D_QKV + 3 * D_RWKV:], D_LORA_PAD)
    g_pm = g_pre_mix[l][None]
    mu = mu_shift[l]
    mu_rkv = mu[None, :3 * D_RWKV]
    mu_lora = _pad_cols(mu[None, 3 * D_RWKV:], D_LORA_PAD)
    wl = jnp.zeros((D_LORA_PAD, 3 * D_RWKV), f32)
    wl = wl.at[:D_DECAY_LORA, :D_RWKV].set(w_decay_up[l])
    wl = wl.at[D_DECAY_LORA:D_DECAY_LORA + D_A_LORA, D_RWKV:2 * D_RWKV].set(w_a_up[l])
    wl = wl.at[D_DECAY_LORA + D_A_LORA:D_LORA, 2 * D_RWKV:].set(w_g_up[l])
    wl = wl.astype(bf16)
    hid = jnp.arange(D_RWKV) // HEAD_DIM
    bd = (hid[:, None] == hid[None, :]).astype(bf16)
    prep_consts = (mu_rkv, mu_lora, wl, w0[l][None], a0[l][None], k_k[l][None], k_a[l][None], bd)
    rk = r_k[l].reshape(1, D_RWKV)
    gnw = gn_w[l][None]
    gnb = gn_b[l][None]
    wo = w_out[l].astype(bf16)
    wfi = w_ffn_in[l].astype(bf16)
    chunked = lambda w: w.reshape(D_MODEL, N_FF_CHUNKS, FF_CHUNK).transpose(1, 0, 2)
    wz = chunked(wfi[:, :D_FF])
    wu = chunked(wfi[:, D_FF:])
    wd = w_ffn_out[l].astype(bf16).reshape(N_FF_CHUNKS, FF_CHUNK, D_MODEL)
    cp = jnp.concatenate([conv_w[l], conv_b[l][None],
                          jnp.zeros((SUBLANES - CONV_W - 1, D_FF), f32)], axis=0)
    cp = cp.reshape(SUBLANES, N_FF_CHUNKS, FF_CHUNK).transpose(1, 0, 2)
    ffn_consts = (wo[:D_ATTN], wo[D_ATTN:], g_post_mix[l][None], g_pre_ffn[l][None],
                  wz, wu, cp, wd, g_post_ffn[l][None])
    sinks = attn_sinks[l]

    xp = x_prompt.reshape(B * T, D_MODEL)
    cos_p, sin_p = _rope_tables(jnp.arange(T, dtype=jnp.int32))
    qkv_p, rkv_p, lora_p = _in_proj(xp, g_pm, wqkv, wrkv, wlora, cos_p, sin_p, tm=512)
    oa_p = _attn_prompt(sinks, qkv_p.reshape(B, T, D_QKV), blk=WINDOW).reshape(B * T, D_ATTN)
    zs_rkv = jnp.zeros((B, 1, 3 * D_RWKV), f32)
    zs_lora = jnp.zeros((B, 1, D_LORA_PAD), f32)
    vecs_p = _rwkv_prep_prompt(rkv_p, lora_p, zs_rkv, zs_lora, prep_consts, B, T, tm=512)
    s0_p = jnp.zeros((B, RWKV_HEADS // 2, LANES, LANES), f32)
    orw_p, st_p = _rwkv_scan(vecs_p, s0_p, rk, gnw, gnb, bd, B, T, tt=256, chunk=RWKV_CHUNK)
    zinit = jnp.zeros((B, N_FF_CHUNKS, SUBLANES, FF_CHUNK), f32)
    y_p, ztail_p = _ffn(xp, oa_p, orw_p, ffn_consts, (zinit,), B, T, tm=512, seq=None)

    qkv_p3 = qkv_p.reshape(B, T, D_QKV)
    new_k_p = qkv_p3[:, T - WINDOW:, D_ATTN:D_ATTN + D_KV].reshape(B, WINDOW, N_KV_HEADS, HEAD_DIM)
    new_v_p = qkv_p3[:, T - WINDOW:, D_ATTN + D_KV:].reshape(B, WINDOW, N_KV_HEADS, HEAD_DIM)
    new_shift_p = jnp.concatenate(
        [rkv_p.reshape(B, T, -1)[:, -1], lora_p.reshape(B, T, -1)[:, -1, :D_LORA]], axis=-1)
    new_wkv_p = _from_pairs(st_p)
    new_conv_p = ztail_p[:, :, SUBLANES - (CONV_W - 1):, :].transpose(0, 2, 1, 3).reshape(
        B, CONV_W - 1, D_FF)

    ns = Bd * L
    xs = x_sample.reshape(ns, D_MODEL)
    cos_s, sin_s = _rope_tables(PAST_LEN + jnp.arange(L, dtype=jnp.int32))
    cos_s = jnp.tile(cos_s, (Bd, 1))
    sin_s = jnp.tile(sin_s, (Bd, 1))
    qkv_s, rkv_s, lora_s = _in_proj(xs, g_pm, wqkv, wrkv, wlora, cos_s, sin_s, tm=ns)
    ck = cache_k_win[l].reshape(Bd, WINDOW, D_KV)
    cv = cache_v_win[l].reshape(Bd, WINDOW, D_KV)
    oa_s, nk_s, nv_s = _attn_sample(sinks, qkv_s.reshape(Bd, L, D_QKV), ck, cv, bb=8)
    oa_s = oa_s.reshape(ns, D_ATTN)
    sh = state_shift[l]
    c_rkv = jnp.repeat(sh[:, :3 * D_RWKV], L, axis=0)
    c_lora = jnp.repeat(_pad_cols(sh[:, 3 * D_RWKV:], D_LORA_PAD), L, axis=0)
    vecs_s = _rwkv_prep_sample(rkv_s, lora_s, c_rkv, c_lora, prep_consts, seq=L, tm=ns)
    padt = lambda a: jnp.pad(a.reshape(Bd, L, D_RWKV), ((0, 0), (0, RWKV_CHUNK - L), (0, 0))
                             ).reshape(Bd * RWKV_CHUNK, D_RWKV)
    vecs_s_pad = tuple(padt(a) for a in vecs_s)
    orw_s, st_s = _rwkv_scan(vecs_s_pad, _to_pairs(state_wkv[l]), rk, gnw, gnb, bd,
                             Bd, RWKV_CHUNK, tt=4 * RWKV_CHUNK, chunk=RWKV_CHUNK)
    orw_s = orw_s.reshape(Bd, RWKV_CHUNK, D_RWKV)[:, :L].reshape(ns, D_RWKV)
    sc = state_conv[l]
    cand = lambda a: jnp.repeat(a, L, axis=0).reshape(ns, N_FF_CHUNKS, FF_CHUNK).transpose(1, 0, 2)
    y_s, z_s = _ffn(xs, oa_s, orw_s, ffn_consts, (cand(sc[:, 0]), cand(sc[:, 1])),
                    Bd, L, tm=256, seq=L)
    new_shift_s = jnp.concatenate(
        [rkv_s.reshape(Bd, L, -1)[:, -1], lora_s.reshape(Bd, L, -1)[:, -1, :D_LORA]], axis=-1)
    new_conv_s = z_s.transpose(1, 0, 2).reshape(Bd, L, D_FF)[:, L - (CONV_W - 1):]

    lead = lambda a: a[None]
    return (y_p.reshape(B, T, D_MODEL), y_s.reshape(Bd, L, D_MODEL),
            lead(new_k_p), lead(new_v_p), lead(new_shift_p), lead(new_wkv_p), lead(new_conv_p),
            lead(nk_s.reshape(Bd, WINDOW, N_KV_HEADS, HEAD_DIM)),
            lead(nv_s.reshape(Bd, WINDOW, N_KV_HEADS, HEAD_DIM)),
            lead(new_shift_s), lead(_from_pairs(st_s)), lead(new_conv_s))
```

```python
import functools
import math

import jax
import jax.numpy as jnp
from jax import lax
from jax.experimental import pallas as pl
from jax.experimental.pallas import tpu as pltpu

f32 = jnp.float32
bf16 = jnp.bfloat16

D_MODEL = 1024
HEAD_DIM = 64
D_ATTN = 512
D_RWKV = 512
N_HEADS = 8
N_KV_HEADS = 2
GQA_GROUP = 4
D_KV = 128
WINDOW = 128
ROPE_THETA = 10000.0
RWKV_HEADS = 8
D_DECAY_LORA = 32
D_A_LORA = 32
D_GATE_LORA = 96
D_LORA = D_DECAY_LORA + D_A_LORA + D_GATE_LORA
D_LORA_PAD = 256
D_SHIFT = 3 * D_RWKV + D_LORA
D_QKV = D_ATTN + 2 * D_KV
D_FF = 2816
FF_CHUNK = 256
N_FF_CHUNKS = D_FF // FF_CHUNK
CONV_W = 3
PAST_LEN = 16384
RMS_EPS = 1e-6
GN_EPS = 64e-5
NEG_INF = -1e30
LANES = 128
SUBLANES = 8
RWKV_CHUNK = 64
VMEM_LIMIT = 56 * 1024 * 1024


def _const_spec(shape):
    nd = len(shape)
    return pl.BlockSpec(shape, lambda *_: (0,) * nd, pipeline_mode=pl.Buffered(1))


def _rms(x, g):
    ms = jnp.mean(x * x, axis=-1, keepdims=True)
    return x * lax.rsqrt(ms + RMS_EPS) * g


def _dot(a, b):
    return jnp.dot(a, b, preferred_element_type=f32)


def _dot_nt(a, b):
    return lax.dot_general(a, b, (((1,), (1,)), ((), ())), preferred_element_type=f32)


def _dot_tn(a, b):
    return lax.dot_general(a, b, (((0,), (0,)), ((), ())), preferred_element_type=f32)


def _group_sum(x, bd2):
    hi = x.astype(bf16)
    lo = (x - hi.astype(f32)).astype(bf16)
    return _dot(jnp.concatenate([hi, lo], axis=1), bd2)


def _group_sum_slab(x, bd2):
    return jnp.concatenate([_group_sum(x[:, j:j + LANES], bd2)
                            for j in range(0, x.shape[1], LANES)], axis=1)


def _in_proj_kernel(x_ref, g_ref, wqkv_ref, wrkv_ref, wlora_ref, cos_ref, sin_ref,
                    qkv_ref, rkv_ref, lora_ref):
    hn = _rms(x_ref[...], g_ref[...]).astype(bf16)
    qkv = _dot(hn, wqkv_ref[...])
    cos = cos_ref[...]
    sin = sin_ref[...]
    lane = lax.broadcasted_iota(jnp.int32, cos.shape, 1)
    first_half = (lane % HEAD_DIM) < (HEAD_DIM // 2)
    for j in range((D_ATTN + D_KV) // LANES):
        xs = qkv[:, j * LANES:(j + 1) * LANES]
        partner = jnp.where(first_half,
                            pltpu.roll(xs, LANES - HEAD_DIM // 2, 1),
                            pltpu.roll(xs, HEAD_DIM // 2, 1))
        qkv_ref[:, j * LANES:(j + 1) * LANES] = xs * cos + partner * sin
    qkv_ref[:, D_ATTN + D_KV:] = qkv[:, D_ATTN + D_KV:]
    rkv_ref[...] = _dot(hn, wrkv_ref[...])
    lora_ref[...] = _dot(hn, wlora_ref[...])


def _in_proj(x2d, g, wqkv, wrkv, wlora, cos, sin, tm):
    n = x2d.shape[0]
    n_pos = cos.shape[0] // tm
    row = lambda w: pl.BlockSpec((tm, w), lambda i: (i, 0))
    pos = pl.BlockSpec((tm, LANES), lambda i: (i % n_pos, 0))
    return pl.pallas_call(
        _in_proj_kernel,
        grid=(n // tm,),
        in_specs=[row(D_MODEL), _const_spec((1, D_MODEL)), _const_spec(wqkv.shape),
                  _const_spec(wrkv.shape), _const_spec(wlora.shape), pos, pos],
        out_specs=[row(D_QKV), row(3 * D_RWKV), row(D_LORA_PAD)],
        out_shape=[jax.ShapeDtypeStruct((n, D_QKV), f32),
                   jax.ShapeDtypeStruct((n, 3 * D_RWKV), f32),
                   jax.ShapeDtypeStruct((n, D_LORA_PAD), f32)],
        compiler_params=pltpu.CompilerParams(
            dimension_semantics=("arbitrary",), vmem_limit_bytes=VMEM_LIMIT),
        name="in_proj",
    )(x2d, g, wqkv, wrkv, wlora, cos, sin)


def _softmax_sink(s, sink_col):
    m = jnp.maximum(jnp.max(s, axis=-1, keepdims=True), sink_col)
    p = jnp.exp(s - m)
    den = jnp.sum(p, axis=-1, keepdims=True) + jnp.exp(sink_col - m)
    return p / den


def _sink_column(sink_ref, kh, rows_per_head, shape):
    row = lax.broadcasted_iota(jnp.int32, shape, 0)
    col = jnp.full(shape, sink_ref[kh * GQA_GROUP + GQA_GROUP - 1], f32)
    for g in range(GQA_GROUP - 2, -1, -1):
        col = jnp.where(row < (g + 1) * rows_per_head, sink_ref[kh * GQA_GROUP + g], col)
    return col


def _attn_prompt_kernel(sink_ref, q_ref, kvc_ref, kvp_ref, o_ref):
    blk = q_ref.shape[1]
    i = pl.program_id(1)
    q = q_ref[0] * (HEAD_DIM ** -0.5)
    kvc = kvc_ref[0]
    kvp = kvp_ref[0]
    shape = (GQA_GROUP * blk, 2 * blk)
    qi = lax.broadcasted_iota(jnp.int32, shape, 0) % blk
    sj = lax.broadcasted_iota(jnp.int32, shape, 1)
    dist = qi + blk - sj
    mask = (dist >= 0) & (dist <= WINDOW) & ((sj >= blk) | (i > 0))
    for kh in range(N_KV_HEADS):
        ks = slice(kh * HEAD_DIM, (kh + 1) * HEAD_DIM)
        vs = slice(D_KV + kh * HEAD_DIM, D_KV + (kh + 1) * HEAD_DIM)
        kcat = jnp.concatenate([kvp[:, ks], kvc[:, ks]], axis=0).astype(bf16)
        vcat = jnp.concatenate([kvp[:, vs], kvc[:, vs]], axis=0).astype(bf16)
        qs = jnp.concatenate(
            [q[:, (kh * GQA_GROUP + g) * HEAD_DIM:(kh * GQA_GROUP + g + 1) * HEAD_DIM]
             for g in range(GQA_GROUP)], axis=0).astype(bf16)
        s = jnp.where(mask, _dot_nt(qs, kcat), NEG_INF)
        p = _softmax_sink(s, _sink_column(sink_ref, kh, blk, (GQA_GROUP * blk, 1)))
        o = _dot(p.astype(bf16), vcat)
        for g in range(GQA_GROUP):
            h = kh * GQA_GROUP + g
            o_ref[0, :, h * HEAD_DIM:(h + 1) * HEAD_DIM] = o[g * blk:(g + 1) * blk]


def _attn_prompt(sinks, qkv3, blk):
    b, t, _ = qkv3.shape
    return pl.pallas_call(
        _attn_prompt_kernel,
        grid=(b, t // blk),
        in_specs=[pl.BlockSpec(memory_space=pltpu.SMEM),
                  pl.BlockSpec((1, blk, D_ATTN), lambda bi, i: (bi, i, 0)),
                  pl.BlockSpec((1, blk, 2 * D_KV), lambda bi, i: (bi, i, D_ATTN // (2 * D_KV))),
                  pl.BlockSpec((1, blk, 2 * D_KV),
                               lambda bi, i: (bi, jnp.maximum(i - 1, 0), D_ATTN // (2 * D_KV)))],
        out_specs=pl.BlockSpec((1, blk, D_ATTN), lambda bi, i: (bi, i, 0)),
        out_shape=jax.ShapeDtypeStruct((b, t, D_ATTN), f32),
        compiler_params=pltpu.CompilerParams(
            dimension_semantics=("arbitrary", "arbitrary"), vmem_limit_bytes=VMEM_LIMIT),
        name="attn_prompt",
    )(sinks, qkv3, qkv3, qkv3)


def _attn_sample_kernel(sink_ref, qkv_ref, ck_ref, cv_ref, o_ref, nk_ref, nv_ref):
    bb, L, _ = qkv_ref.shape
    qkv = qkv_ref[...]
    q = qkv[:, :, :D_ATTN] * (HEAD_DIM ** -0.5)
    kn = qkv[:, :, D_ATTN:D_ATTN + D_KV]
    vn = qkv[:, :, D_ATTN + D_KV:]
    ck = ck_ref[...]
    cv = cv_ref[...]
    nk_ref[:, :WINDOW - L, :] = ck[:, L:, :]
    nk_ref[:, WINDOW - L:, :] = kn
    nv_ref[:, :WINDOW - L, :] = cv[:, L:, :]
    nv_ref[:, WINDOW - L:, :] = vn
    rows = GQA_GROUP * L
    qi = lax.broadcasted_iota(jnp.int32, (rows, WINDOW), 0) % L
    sj = lax.broadcasted_iota(jnp.int32, (rows, WINDOW), 1)
    mask_c = (sj >= qi)[None]
    qn = lax.broadcasted_iota(jnp.int32, (rows, L), 0) % L
    jn = lax.broadcasted_iota(jnp.int32, (rows, L), 1)
    mask_n = (jn <= qn)[None]
    for kh in range(N_KV_HEADS):
        ks = slice(kh * HEAD_DIM, (kh + 1) * HEAD_DIM)
        qs = jnp.concatenate(
            [q[:, :, (kh * GQA_GROUP + g) * HEAD_DIM:(kh * GQA_GROUP + g + 1) * HEAD_DIM]
             for g in range(GQA_GROUP)], axis=1)
        qsb = qs.astype(bf16)
        s_c = jnp.einsum('bqd,bkd->bqk', qsb, ck[:, :, ks].astype(bf16),
                         preferred_element_type=f32)
        s_n = jnp.einsum('bqd,bkd->bqk', qsb.astype(f32), kn[:, :, ks].astype(bf16).astype(f32),
                         preferred_element_type=f32, precision=lax.Precision.HIGHEST)
        s_c = jnp.where(mask_c, s_c, NEG_INF)
        s_n = jnp.where(mask_n, s_n, NEG_INF)
        sink_col = _sink_column(sink_ref, kh, L, (rows, 1))[None]
        m = jnp.maximum(jnp.maximum(jnp.max(s_c, axis=-1, keepdims=True),
                                    jnp.max(s_n, axis=-1, keepdims=True)), sink_col)
        p_c = jnp.exp(s_c - m)
        p_n = jnp.exp(s_n - m)
        den = (jnp.sum(p_c, axis=-1, keepdims=True) + jnp.sum(p_n, axis=-1, keepdims=True)
               + jnp.exp(sink_col - m))
        p_c = (p_c / den).astype(bf16)
        p_n = (p_n / den).astype(bf16).astype(f32)
        o = jnp.einsum('bqk,bkd->bqd', p_c, cv[:, :, ks].astype(bf16), preferred_element_type=f32)
        o = o + jnp.einsum('bqk,bkd->bqd', p_n, vn[:, :, ks].astype(bf16).astype(f32),
                           preferred_element_type=f32, precision=lax.Precision.HIGHEST)
        for g in range(GQA_GROUP):
            h = kh * GQA_GROUP + g
            o_ref[:, :, h * HEAD_DIM:(h + 1) * HEAD_DIM] = o[:, g * L:(g + 1) * L, :]


def _attn_sample(sinks, qkv3, ck, cv, bb):
    b, L, _ = qkv3.shape
    blk3 = lambda r, w: pl.BlockSpec((bb, r, w), lambda i: (i, 0, 0))
    return pl.pallas_call(
        _attn_sample_kernel,
        grid=(b // bb,),
        in_specs=[pl.BlockSpec(memory_space=pltpu.SMEM), blk3(L, D_QKV),
                  blk3(WINDOW, D_KV), blk3(WINDOW, D_KV)],
        out_specs=[blk3(L, D_ATTN), blk3(WINDOW, D_KV), blk3(WINDOW, D_KV)],
        out_shape=[jax.ShapeDtypeStruct((b, L, D_ATTN), f32),
                   jax.ShapeDtypeStruct((b, WINDOW, D_KV), f32),
                   jax.ShapeDtypeStruct((b, WINDOW, D_KV), f32)],
        compiler_params=pltpu.CompilerParams(
            dimension_semantics=("arbitrary",), vmem_limit_bytes=VMEM_LIMIT),
        name="attn_sample",
    )(sinks, qkv3, ck, cv)


def _rwkv_prep_body(h_rkv, h_lora, p_rkv, p_lora, mu_rkv, mu_lora, wl, w0, a0, kk_w, ka_w, bd,
                    r_ref, lw_ref, k_ref, v_ref, kk_ref, ba_ref, g_ref):
    hs = h_rkv + (p_rkv - h_rkv) * mu_rkv
    ls = h_lora + (p_lora - h_lora) * mu_lora
    r = hs[:, :D_RWKV]
    k = hs[:, D_RWKV:2 * D_RWKV]
    v = hs[:, 2 * D_RWKV:]
    lane = lax.broadcasted_iota(jnp.int32, ls.shape, 1)
    act = jnp.where(lane < D_DECAY_LORA, jnp.tanh(ls),
                    jnp.where(lane < D_DECAY_LORA + D_A_LORA, ls, jax.nn.sigmoid(ls)))
    pre = _dot(act.astype(bf16), wl)
    xw = -(w0 + pre[:, :D_RWKV])
    softplus = jnp.maximum(xw, 0.0) + jnp.log1p(jnp.exp(-jnp.abs(xw)))
    w = -softplus - 0.5
    lw_ref[...] = -jnp.exp(w)
    a = jax.nn.sigmoid(a0 + pre[:, D_RWKV:2 * D_RWKV])
    g_ref[...] = pre[:, 2 * D_RWKV:]
    kkp = k * kk_w
    kk = kkp * lax.rsqrt(jnp.maximum(_group_sum_slab(kkp * kkp, bd), 1e-24))
    r_ref[...] = r
    k_ref[...] = k * (1.0 + (a - 1.0) * ka_w)
    v_ref[...] = v
    kk_ref[...] = kk
    ba_ref[...] = kk * a


def _shift_rows(x, first_row):
    row = lax.broadcasted_iota(jnp.int32, x.shape, 0)
    return jnp.where(row == 0, first_row, pltpu.roll(x, 1, 0))


def _rwkv_prep_prompt_kernel(rkv_ref, lora_ref, prkv_ref, plora_ref, s_rkv_ref, s_lora_ref,
                             mu_rkv_ref, mu_lora_ref, wl_ref, w0_ref, a0_ref, kkw_ref, kaw_ref,
                             bd_ref, *out_refs):
    i = pl.program_id(1)
    h_rkv = rkv_ref[...]
    h_lora = lora_ref[...]
    first = i == 0
    prev_rkv = jnp.where(first, s_rkv_ref[0], prkv_ref[SUBLANES - 1:SUBLANES, :])
    prev_lora = jnp.where(first, s_lora_ref[0], plora_ref[SUBLANES - 1:SUBLANES, :])
    _rwkv_prep_body(h_rkv, h_lora, _shift_rows(h_rkv, prev_rkv), _shift_rows(h_lora, prev_lora),
                    mu_rkv_ref[...], mu_lora_ref[...], wl_ref[...], w0_ref[...], a0_ref[...],
                    kkw_ref[...], kaw_ref[...], bd_ref[...], *out_refs)


def _rwkv_prep_sample_kernel(rkv_ref, lora_ref, c_rkv_ref, c_lora_ref, mu_rkv_ref, mu_lora_ref,
                             wl_ref, w0_ref, a0_ref, kkw_ref, kaw_ref, bd_ref, *out_refs, seq):
    h_rkv = rkv_ref[...]
    h_lora = lora_ref[...]

    def prev(h, cand):
        t = lax.broadcasted_iota(jnp.int32, h.shape, 0) % seq
        return jnp.where(t == 0, cand, pltpu.roll(h, 1, 0))

    _rwkv_prep_body(h_rkv, h_lora, prev(h_rkv, c_rkv_ref[...]), prev(h_lora, c_lora_ref[...]),
                    mu_rkv_ref[...], mu_lora_ref[...], wl_ref[...], w0_ref[...], a0_ref[...],
                    kkw_ref[...], kaw_ref[...], bd_ref[...], *out_refs)


def _rwkv_prep_prompt(rkv, lora, s_rkv, s_lora, consts, b, t, tm):
    nt = t // tm
    per8 = tm // SUBLANES
    row = lambda w: pl.BlockSpec((tm, w), lambda bi, i: (bi * nt + i, 0))
    prow = lambda w: pl.BlockSpec(
        (SUBLANES, w), lambda bi, i: (jnp.maximum((bi * nt + i) * per8 - 1, 0), 0))
    st = lambda w: pl.BlockSpec((1, 1, w), lambda bi, i: (bi, 0, 0))
    n = b * t
    return pl.pallas_call(
        _rwkv_prep_prompt_kernel,
        grid=(b, nt),
        in_specs=[row(3 * D_RWKV), row(D_LORA_PAD), prow(3 * D_RWKV), prow(D_LORA_PAD),
                  st(3 * D_RWKV), st(D_LORA_PAD)] + [_const_spec(c.shape) for c in consts],
        out_specs=[row(D_RWKV)] * 7,
        out_shape=[jax.ShapeDtypeStruct((n, D_RWKV), f32)] * 7,
        compiler_params=pltpu.CompilerParams(
            dimension_semantics=("arbitrary", "arbitrary"), vmem_limit_bytes=VMEM_LIMIT),
        name="rwkv_prep_prompt",
    )(rkv, lora, rkv, lora, s_rkv, s_lora, *consts)


def _rwkv_prep_sample(rkv, lora, c_rkv, c_lora, consts, seq, tm):
    n = rkv.shape[0]
    row = lambda w: pl.BlockSpec((tm, w), lambda i: (i, 0))
    return pl.pallas_call(
        functools.partial(_rwkv_prep_sample_kernel, seq=seq),
        grid=(n // tm,),
        in_specs=[row(3 * D_RWKV), row(D_LORA_PAD), row(3 * D_RWKV), row(D_LORA_PAD)]
        + [_const_spec(c.shape) for c in consts],
        out_specs=[row(D_RWKV)] * 7,
        out_shape=[jax.ShapeDtypeStruct((n, D_RWKV), f32)] * 7,
        compiler_params=pltpu.CompilerParams(
            dimension_semantics=("arbitrary",), vmem_limit_bytes=VMEM_LIMIT),
        name="rwkv_prep_sample",
    )(rkv, lora, c_rkv, c_lora, *consts)


def _interleave(gens):
    active = list(gens)
    while active:
        still = []
        for gen in active:
            try:
                next(gen)
                still.append(gen)
            except StopIteration:
                pass
        active = still


def _rwkv_scan_kernel(r_ref, lw_ref, k_ref, v_ref, kk_ref, ba_ref, g_ref, s0_ref,
                      rk_ref, gnw_ref, gnb_ref, bd_ref, o_ref, st_ref, *scratch, chunk, chained):
    tt = r_ref.shape[0]
    nch = tt // chunk
    n_pairs = D_RWKV // LANES
    n_factors = int(math.log2(chunk))
    assert chunk == HEAD_DIM and 2 * chunk == LANES
    if chained:
        s_scr, y_scr = scratch

        @pl.when(pl.program_id(1) == 0)
        def _():
            s_scr[...] = s0_ref[0]
    else:
        (y_scr,) = scratch

    ti = lax.broadcasted_iota(jnp.int32, (chunk, chunk), 0)
    tj = lax.broadcasted_iota(jnp.int32, (chunk, chunk), 1)
    tri = (ti >= tj).astype(bf16)
    ri = lax.broadcasted_iota(jnp.int32, (chunk, LANES), 0)
    rj = lax.broadcasted_iota(jnp.int32, (chunk, LANES), 1) % chunk
    strict = ri > rj
    incl = ri >= rj
    eye_sbs = (ri == rj).astype(f32)
    li = lax.broadcasted_iota(jnp.int32, (LANES, LANES), 0)
    lj = lax.broadcasted_iota(jnp.int32, (LANES, LANES), 1)
    eye_l = li == lj
    same_head = (li // HEAD_DIM) == (lj // HEAD_DIM)
    low_half = lax.broadcasted_iota(jnp.int32, (chunk, LANES), 1) < HEAD_DIM

    def stack(x):
        zero = jnp.zeros_like(x)
        return jnp.concatenate([jnp.where(low_half, x, zero), jnp.where(low_half, zero, x)], axis=0)

    prep = {}
    res = {}
    state = {p: s_scr[p] for p in range(n_pairs)} if chained else {}

    def prep_gen(c):
        rows = slice(c * chunk, (c + 1) * chunk)
        lw = lw_ref[rows, :]
        hi = lw.astype(bf16)
        r1 = lw - hi.astype(f32)
        mid = r1.astype(bf16)
        lo = (r1 - mid.astype(f32)).astype(bf16)
        cl = _dot(tri, hi) + _dot(tri, mid) + _dot(tri, lo)
        yield
        e_fwd = jnp.exp(cl)
        e_inv = jnp.exp(-cl)
        prep[c] = (r_ref[rows, :] * e_fwd, -kk_ref[rows, :] * jnp.exp(cl - lw),
                   ba_ref[rows, :] * e_inv, k_ref[rows, :] * e_inv, v_ref[rows, :],
                   e_fwd[chunk - 1:chunk, :])

    def inst_gen(c, p):
        rt_all, at_all, bt_all, kt_all, v_all, pc = prep[c]
        ls = slice(p * LANES, (p + 1) * LANES)
        rt_f = rt_all[:, ls]
        at = at_all[:, ls].astype(bf16)
        bt_f = bt_all[:, ls]
        kt_f = kt_all[:, ls]
        pcp = pc[:, ls]
        bh = (bt_f * pcp).astype(bf16)
        kh = (kt_f * pcp).astype(bf16)
        vb = v_all[:, ls].astype(bf16)
        vs = stack(vb)
        aa = _dot_nt(jnp.concatenate([at, rt_f.astype(bf16)], axis=0),
                     jnp.concatenate([stack(bt_f.astype(bf16)), stack(kt_f.astype(bf16))], axis=0))
        yield
        lab = jnp.where(strict, aa[:chunk, :LANES], 0.0)
        lak = jnp.where(strict, aa[:chunk, LANES:], 0.0).astype(bf16)
        arb = jnp.where(incl, aa[chunk:, :LANES], 0.0).astype(bf16)
        ark = jnp.where(incl, aa[chunk:, LANES:], 0.0).astype(bf16)
        lab_b = lab.astype(bf16)
        lp = _dot(lab_b, stack(lab_b))
        x = _dot(lak, vs).astype(bf16)
        arkv = _dot(ark, vs)
        vtk = _dot_tn(vb, kh)
        tm = eye_sbs + lab
        yield
        for _ in range(n_factors - 2):
            lpb = lp.astype(bf16)
            prod = _dot(jnp.concatenate([tm.astype(bf16), lpb], axis=0), stack(lpb))
            tm = tm + prod[:chunk]
            lp = prod[chunk:]
            yield
        tm = tm + _dot(tm.astype(bf16), stack(lp.astype(bf16)))
        yield
        wu = _dot(tm.astype(bf16), jnp.concatenate([stack(at), stack(x)], axis=1)).astype(bf16)
        yield
        mg = _dot_tn(wu, bh)
        qy = _dot(arb, jnp.concatenate([stack(wu[:, :LANES]), stack(wu[:, LANES:])], axis=1))
        m = jnp.where(eye_l, pcp, 0.0) + jnp.where(same_head, mg[:LANES], 0.0)
        gsum = mg[LANES:] + vtk
        g_slab = jnp.where(low_half, gsum[:HEAD_DIM], gsum[HEAD_DIM:])
        res[(c, p)] = ((rt_f + qy[:, :LANES]).astype(bf16), qy[:, LANES:] + arkv,
                       m.astype(bf16), g_slab)

    def chain_gen(chunks):
        for c in chunks:
            rows = slice(c * chunk, (c + 1) * chunk)
            for p in range(n_pairs):
                qb, y0, mb, g_slab = res.pop((c, p))
                sb = (state[p] if chained else s0_ref[c, p]).astype(bf16)
                y_scr[rows, p * LANES:(p + 1) * LANES] = _dot_nt(qb, stack(sb)) + y0
                s_new = _dot(sb, mb) + g_slab
                if chained:
                    state[p] = s_new
                else:
                    st_ref[c, p] = s_new
            yield

    _interleave([prep_gen(c) for c in range(nch)])
    group = 4
    prev = None
    for g0 in range(0, nch, group):
        cur = list(range(g0, min(g0 + group, nch)))
        gens = [inst_gen(c, p) for c in cur for p in range(n_pairs)]
        if prev is not None:
            gens.append(chain_gen(prev))
        _interleave(gens)
        prev = cur
    _interleave([chain_gen(prev)])
    if chained:
        for p in range(n_pairs):
            s_scr[p] = state[p]

    bd = bd_ref[...]
    inv_n = 1.0 / HEAD_DIM
    for p in range(n_pairs):
        ls = slice(p * LANES, (p + 1) * LANES)
        y = y_scr[:, ls]
        mean = _group_sum(y, bd) * inv_n
        d = y - mean
        var = _group_sum(d * d, bd) * inv_n
        yn = d * lax.rsqrt(var + GN_EPS) * gnw_ref[:, ls] + gnb_ref[:, ls]
        bonus = _group_sum(r_ref[:, ls] * k_ref[:, ls] * rk_ref[:, ls], bd) * v_ref[:, ls]
        o_ref[:, ls] = (yn + bonus) * g_ref[:, ls]

    if chained:
        @pl.when(pl.program_id(1) == pl.num_programs(1) - 1)
        def _():
            st_ref[0] = s_scr[...]


def _rwkv_scan(vecs, s0_bd, rk, gnw, gnb, bd, b, t, tt, chunk):
    n_pairs = D_RWKV // LANES
    chained = t != chunk
    if chained:
        nt = t // tt
        grid = (b, nt)
        row = pl.BlockSpec((tt, D_RWKV), lambda bi, i: (bi * nt + i, 0))
        stspec = pl.BlockSpec((1, n_pairs, HEAD_DIM, LANES), lambda bi, i: (bi, 0, 0, 0))
        scratch = [pltpu.VMEM((n_pairs, HEAD_DIM, LANES), f32)]
        sem = ("arbitrary", "arbitrary")
    else:
        nseq = tt // chunk
        grid = (b // nseq,)
        row = pl.BlockSpec((tt, D_RWKV), lambda i: (i, 0))
        stspec = pl.BlockSpec((nseq, n_pairs, HEAD_DIM, LANES), lambda i: (i, 0, 0, 0))
        scratch = []
        sem = ("arbitrary",)
    return pl.pallas_call(
        functools.partial(_rwkv_scan_kernel, chunk=chunk, chained=chained),
        grid=grid,
        in_specs=[row] * 7 + [stspec, _const_spec(rk.shape), _const_spec(gnw.shape),
                              _const_spec(gnb.shape), _const_spec(bd.shape)],
        out_specs=[row, stspec],
        out_shape=[jax.ShapeDtypeStruct((b * t, D_RWKV), f32),
                   jax.ShapeDtypeStruct((b, n_pairs, HEAD_DIM, LANES), f32)],
        scratch_shapes=scratch + [pltpu.VMEM((tt, D_RWKV), f32)],
        compiler_params=pltpu.CompilerParams(dimension_semantics=sem, vmem_limit_bytes=VMEM_LIMIT),
        name="rwkv_scan",
    )(*vecs, s0_bd, rk, gnw, gnb, bd)


def _ffn_kernel(x_ref, oa_ref, orw_ref, woa_ref, wor_ref, gpm_ref, gpf_ref, wz_ref, wu_ref,
                cp_ref, wd_ref, gpo_ref, *rest, seq):
    if seq is None:
        zinit_ref, out_ref, ztail_ref, x1_scr, hn_scr, f_scr, zprev_scr = rest
    else:
        ca_ref, cb_ref, out_ref, ztail_ref, x1_scr, hn_scr, f_scr = rest
    tm = x_ref.shape[0]
    mix = _dot(oa_ref[...].astype(bf16), woa_ref[...]) + _dot(orw_ref[...].astype(bf16), wor_ref[...])
    x1 = x_ref[...] + _rms(mix, gpm_ref[...])
    x1_scr[...] = x1
    hn_scr[...] = _rms(x1, gpf_ref[...]).astype(bf16)
    f_scr[...] = jnp.zeros_like(f_scr)
    if seq is None:
        @pl.when(pl.program_id(1) == 0)
        def _():
            zprev_scr[...] = zinit_ref[0]
    row = lax.broadcasted_iota(jnp.int32, (tm, FF_CHUNK), 0)
    tmod = row if seq is None else row % seq

    zu = {}

    def ff_in(c):
        hn = hn_scr[...]
        zu[c] = (_dot(hn, wz_ref[c]), _dot(hn, wu_ref[c]))

    def ff_out(c):
        z, u = zu.pop(c)
        cp = cp_ref[c]
        if seq is None:
            tail = zprev_scr[c]
            back2 = tail[SUBLANES - 2:SUBLANES - 1, :]
            back1 = tail[SUBLANES - 1:SUBLANES, :]
            zprev_scr[c] = z[tm - SUBLANES:, :]
            ztail_ref[0, c] = z[tm - SUBLANES:, :]
        else:
            back2 = ca_ref[c]
            back1 = cb_ref[c]
            ztail_ref[c] = z
        z1 = jnp.where(tmod == 0, back1, pltpu.roll(z, 1, 0))
        z2 = jnp.where(tmod == 0, back2, jnp.where(tmod == 1, back1, pltpu.roll(z, 2, 0)))
        zc = cp[3:4, :] + cp[0:1, :] * z2
        zc = zc + cp[1:2, :] * z1
        zc = zc + cp[2:3, :] * z
        hid = zc * jax.nn.sigmoid(zc) * u
        f_scr[...] += _dot(hid.astype(bf16), wd_ref[c])

    ff_in(0)
    for c in range(N_FF_CHUNKS):
        if c + 1 < N_FF_CHUNKS:
            ff_in(c + 1)
        ff_out(c)
    out_ref[...] = x1_scr[...] + _rms(f_scr[...], gpo_ref[...])


def _ffn(x2d, oa, orw, consts, extra, b, t, tm, seq):
    woa, wor, gpm, gpf, wz, wu, cp, wd, gpo = consts
    n = b * t
    if seq is None:
        nt = t // tm
        grid = (b, nt)
        row = lambda w: pl.BlockSpec((tm, w), lambda bi, i: (bi * nt + i, 0))
        tail_block = (1, N_FF_CHUNKS, SUBLANES, FF_CHUNK)
        extra_specs = [pl.BlockSpec(tail_block, lambda bi, i: (bi, 0, 0, 0))]
        tail_spec = pl.BlockSpec(tail_block, lambda bi, i: (bi, 0, 0, 0))
        tail_shape = jax.ShapeDtypeStruct((b,) + tail_block[1:], f32)
        scratch = [pltpu.VMEM((N_FF_CHUNKS, SUBLANES, FF_CHUNK), f32)]
        sem = ("arbitrary", "arbitrary")
    else:
        grid = (n // tm,)
        row = lambda w: pl.BlockSpec((tm, w), lambda i: (i, 0))
        cand = pl.BlockSpec((N_FF_CHUNKS, tm, FF_CHUNK), lambda i: (0, i, 0))
        extra_specs = [cand, cand]
        tail_spec = cand
        tail_shape = jax.ShapeDtypeStruct((N_FF_CHUNKS, n, FF_CHUNK), f32)
        scratch = []
        sem = ("arbitrary",)
    return pl.pallas_call(
        functools.partial(_ffn_kernel, seq=seq),
        grid=grid,
        in_specs=[row(D_MODEL), row(D_ATTN), row(D_RWKV)]
        + [_const_spec(c.shape) for c in consts] + extra_specs,
        out_specs=[row(D_MODEL), tail_spec],
        out_shape=[jax.ShapeDtypeStruct((n, D_MODEL), f32), tail_shape],
        scratch_shapes=[pltpu.VMEM((tm, D_MODEL), f32), pltpu.VMEM((tm, D_MODEL), bf16),
                        pltpu.VMEM((tm, D_MODEL), f32)] + scratch,
        compiler_params=pltpu.CompilerParams(dimension_semantics=sem, vmem_limit_bytes=VMEM_LIMIT),
        name="ffn_prompt" if seq is None else "ffn_sample",
    )(x2d, oa, orw, *consts, *extra)


def _rope_tables(pos):
    half = HEAD_DIM // 2
    inv = ROPE_THETA ** (-jnp.arange(half, dtype=f32) / half)
    ang = pos.astype(f32)[:, None] * inv[None, :]
    cos = jnp.cos(ang)
    sin = jnp.sin(ang)
    reps = LANES // HEAD_DIM
    cos_t = jnp.tile(jnp.concatenate([cos, cos], axis=-1), (1, reps))
    sin_t = jnp.tile(jnp.concatenate([-sin, sin], axis=-1), (1, reps))
    return cos_t, sin_t


def _pad_cols(x, width):
    return jnp.pad(x, ((0, 0),) * (x.ndim - 1) + ((0, width - x.shape[-1]),))


def _to_pairs(s):
    b = s.shape[0]
    s = s.reshape(b, RWKV_HEADS // 2, 2, HEAD_DIM, HEAD_DIM)
    return s.transpose(0, 1, 3, 2, 4).reshape(b, RWKV_HEADS // 2, HEAD_DIM, LANES)


def _from_pairs(s):
    b = s.shape[0]
    s = s.reshape(b, RWKV_HEADS // 2, HEAD_DIM, 2, HEAD_DIM)
    return s.transpose(0, 1, 3, 2, 4).reshape(b, RWKV_HEADS, HEAD_DIM, HEAD_DIM)


def kernel(x_prompt, x_sample, cache_k_win, cache_v_win, state_shift, state_wkv, state_conv,
           g_pre_mix, w_in, attn_sinks, mu_shift, w0, w_decay_up, a0, w_a_up, w_g_up,
           k_k, k_a, r_k, gn_w, gn_b, w_out, g_post_mix, g_pre_ffn, w_ffn_in, conv_w,
           conv_b, w_ffn_out, g_post_ffn):
    B, T, _ = x_prompt.shape
    Bd, L, _ = x_sample.shape
    l = 0
    win = w_in[l].astype(bf16)
    wqkv = win[:, :D_QKV]
    wrkv = win[:, D_QKV:D_QKV + 3 * D_RWKV]
    wlora = _pad_cols(win[:, D_QKV + 3 * D_RWKV:], D_LORA_PAD)
    g_pm = g_pre_mix[l][None]
    mu = mu_shift[l]
    mu_rkv = mu[None, :3 * D_RWKV]
    mu_lora = _pad_cols(mu[None, 3 * D_RWKV:], D_LORA_PAD)
    wl = jnp.zeros((D_LORA_PAD, 3 * D_RWKV), f32)
    wl = wl.at[:D_DECAY_LORA, :D_RWKV].set(w_decay_up[l])
    wl = wl.at[D_DECAY_LORA:D_DECAY_LORA + D_A_LORA, D_RWKV:2 * D_RWKV].set(w_a_up[l])
    wl = wl.at[D_DECAY_LORA + D_A_LORA:D_LORA, 2 * D_RWKV:].set(w_g_up[l])
    wl = wl.astype(bf16)
    hid = jnp.arange(LANES) // HEAD_DIM
    bd = (hid[:, None] == hid[None, :]).astype(bf16)
    bd = jnp.concatenate([bd, bd], axis=0)
    prep_consts = (mu_rkv, mu_lora, wl, w0[l][None], a0[l][None], k_k[l][None], k_a[l][None], bd)
    rk = r_k[l].reshape(1, D_RWKV)
    gnw = gn_w[l][None]
    gnb = gn_b[l][None]
    wo = w_out[l].astype(bf16)
    wfi = w_ffn_in[l].astype(bf16)
    chunked = lambda w: w.reshape(D_MODEL, N_FF_CHUNKS, FF_CHUNK).transpose(1, 0, 2)
    wz = chunked(wfi[:, :D_FF])
    wu = chunked(wfi[:, D_FF:])
    wd = w_ffn_out[l].astype(bf16).reshape(N_FF_CHUNKS, FF_CHUNK, D_MODEL)
    cp = jnp.concatenate([conv_w[l], conv_b[l][None],
                          jnp.zeros((SUBLANES - CONV_W - 1, D_FF), f32)], axis=0)
    cp = cp.reshape(SUBLANES, N_FF_CHUNKS, FF_CHUNK).transpose(1, 0, 2)
    ffn_consts = (wo[:D_ATTN], wo[D_ATTN:], g_post_mix[l][None], g_pre_ffn[l][None],
                  wz, wu, cp, wd, g_post_ffn[l][None])
    sinks = attn_sinks[l]

    xp = x_prompt.reshape(B * T, D_MODEL)
    cos_p, sin_p = _rope_tables(jnp.arange(T, dtype=jnp.int32))
    qkv_p, rkv_p, lora_p = _in_proj(xp, g_pm, wqkv, wrkv, wlora, cos_p, sin_p, tm=512)
    oa_p = _attn_prompt(sinks, qkv_p.reshape(B, T, D_QKV), blk=WINDOW).reshape(B * T, D_ATTN)
    zs_rkv = jnp.zeros((B, 1, 3 * D_RWKV), f32)
    zs_lora = jnp.zeros((B, 1, D_LORA_PAD), f32)
    vecs_p = _rwkv_prep_prompt(rkv_p, lora_p, zs_rkv, zs_lora, prep_consts, B, T, tm=512)
    s0_p = jnp.zeros((B, RWKV_HEADS // 2, HEAD_DIM, LANES), f32)
    orw_p, st_p = _rwkv_scan(vecs_p, s0_p, rk, gnw, gnb, bd, B, T, tt=512, chunk=RWKV_CHUNK)
    zinit = jnp.zeros((B, N_FF_CHUNKS, SUBLANES, FF_CHUNK), f32)
    y_p, ztail_p = _ffn(xp, oa_p, orw_p, ffn_consts, (zinit,), B, T, tm=512, seq=None)

    qkv_p3 = qkv_p.reshape(B, T, D_QKV)
    new_k_p = qkv_p3[:, T - WINDOW:, D_ATTN:D_ATTN + D_KV].reshape(B, WINDOW, N_KV_HEADS, HEAD_DIM)
    new_v_p = qkv_p3[:, T - WINDOW:, D_ATTN + D_KV:].reshape(B, WINDOW, N_KV_HEADS, HEAD_DIM)
    new_shift_p = jnp.concatenate(
        [rkv_p.reshape(B, T, -1)[:, -1], lora_p.reshape(B, T, -1)[:, -1, :D_LORA]], axis=-1)
    new_wkv_p = _from_pairs(st_p)
    new_conv_p = ztail_p[:, :, SUBLANES - (CONV_W - 1):, :].transpose(0, 2, 1, 3).reshape(
        B, CONV_W - 1, D_FF)

    ns = Bd * L
    xs = x_sample.reshape(ns, D_MODEL)
    cos_s, sin_s = _rope_tables(PAST_LEN + jnp.arange(L, dtype=jnp.int32))
    cos_s = jnp.tile(cos_s, (Bd, 1))
    sin_s = jnp.tile(sin_s, (Bd, 1))
    qkv_s, rkv_s, lora_s = _in_proj(xs, g_pm, wqkv, wrkv, wlora, cos_s, sin_s, tm=ns)
    ck = cache_k_win[l].reshape(Bd, WINDOW, D_KV)
    cv = cache_v_win[l].reshape(Bd, WINDOW, D_KV)
    oa_s, nk_s, nv_s = _attn_sample(sinks, qkv_s.reshape(Bd, L, D_QKV), ck, cv, bb=8)
    oa_s = oa_s.reshape(ns, D_ATTN)
    sh = state_shift[l]
    c_rkv = jnp.repeat(sh[:, :3 * D_RWKV], L, axis=0)
    c_lora = jnp.repeat(_pad_cols(sh[:, 3 * D_RWKV:], D_LORA_PAD), L, axis=0)
    vecs_s = _rwkv_prep_sample(rkv_s, lora_s, c_rkv, c_lora, prep_consts, seq=L, tm=ns)
    padt = lambda a: jnp.pad(a.reshape(Bd, L, D_RWKV), ((0, 0), (0, RWKV_CHUNK - L), (0, 0))
                             ).reshape(Bd * RWKV_CHUNK, D_RWKV)
    vecs_s_pad = tuple(padt(a) for a in vecs_s)
    orw_s, st_s = _rwkv_scan(vecs_s_pad, _to_pairs(state_wkv[l]), rk, gnw, gnb, bd,
                             Bd, RWKV_CHUNK, tt=4 * RWKV_CHUNK, chunk=RWKV_CHUNK)
    orw_s = orw_s.reshape(Bd, RWKV_CHUNK, D_RWKV)[:, :L].reshape(ns, D_RWKV)
    sc = state_conv[l]
    cand = lambda a: jnp.repeat(a, L, axis=0).reshape(ns, N_FF_CHUNKS, FF_CHUNK).transpose(1, 0, 2)
    y_s, z_s = _ffn(xs, oa_s, orw_s, ffn_consts, (cand(sc[:, 0]), cand(sc[:, 1])),
                    Bd, L, tm=256, seq=L)
    new_shift_s = jnp.concatenate(
        [rkv_s.reshape(Bd, L, -1)[:, -1], lora_s.reshape(Bd, L, -1)[:, -1, :D_LORA]], axis=-1)
    new_conv_s = z_s.transpose(1, 0, 2).reshape(Bd, L, D_FF)[:, L - (CONV_W - 1):]

    lead = lambda a: a[None]
    return (y_p.reshape(B, T, D_MODEL), y_s.reshape(Bd, L, D_MODEL),
            lead(new_k_p), lead(new_v_p), lead(new_shift_p), lead(new_wkv_p), lead(new_conv_p),
            lead(nk_s.reshape(Bd, WINDOW, N_KV_HEADS, HEAD_DIM)),
            lead(nv_s.reshape(Bd, WINDOW, N_KV_HEADS, HEAD_DIM)),
            lead(new_shift_s), lead(_from_pairs(st_s)), lead(new_conv_s))
```
